```python
import math
import jax, jax.numpy as jnp
from jax import lax
import numpy as np

D_MODEL = 1024
BATCH = 32
SEQ = 2048
DEPTH = 1

MEM_LEN = 256
MEM_HEADS = 4
MEM_HEAD_DIM = 128
GM_WIDTH = D_MODEL // 2
GM_CHUNK = 128
GM_GROUPS = 4
GM_GROUP_W = GM_WIDTH // GM_GROUPS
MLA_HEADS = 8
MLA_NOPE = 128
MLA_ROPE = 64
MLA_V = 128
Q_LORA = 384
KV_LORA = 256
ROPE_BASE = 10000.0
Q_BLOCK = 128
D_FF = 4 * D_MODEL
N_BRANCH = 3
EPS = 1e-6
W_GM = 2 * GM_WIDTH
W_MLA = Q_LORA + KV_LORA + MLA_ROPE
W_MEMQ = MEM_HEADS * MEM_HEAD_DIM
W_GATE = N_BRANCH * D_MODEL
W_IN_COLS = W_GM + W_MLA + W_MEMQ + W_GATE

kernel_name = "hybrid_gmlp_mla_memory_gated_block"


def rmsnorm(x, g):
    xf = x.astype(jnp.float32)
    y = xf * lax.rsqrt(jnp.mean(xf * xf, axis=-1, keepdims=True) + EPS)
    return (y * g.astype(jnp.float32)).astype(x.dtype)


def layernorm(x, g, b):
    xf = x.astype(jnp.float32)
    mu = jnp.mean(xf, axis=-1, keepdims=True)
    xc = xf - mu
    y = xc * lax.rsqrt(jnp.mean(xc * xc, axis=-1, keepdims=True) + EPS)
    return (y * g.astype(jnp.float32) + b.astype(jnp.float32)).astype(x.dtype)


def rope_tables(positions):
    inv_freq = ROPE_BASE ** (-jnp.arange(0, MLA_ROPE, 2, dtype=jnp.float32) / MLA_ROPE)
    ang = positions.astype(jnp.float32)[..., None] * inv_freq
    return jnp.cos(ang), jnp.sin(ang)


def apply_rope(x, cos, sin):
    x1, x2 = jnp.split(x.astype(jnp.float32), 2, axis=-1)
    return jnp.concatenate([x1 * cos - x2 * sin, x2 * cos + x1 * sin], axis=-1).astype(x.dtype)


def gmlp_branch(z_u, z_v, g_ln, b_ln, w_s, b_s):
    B, S, _ = z_u.shape
    u = jax.nn.gelu(z_u)
    v = layernorm(jax.nn.gelu(z_v), g_ln, b_ln)
    v5 = v.reshape(B, S // GM_CHUNK, GM_CHUNK, GM_GROUPS, GM_GROUP_W)
    w_causal = jnp.tril(w_s).astype(v.dtype)
    mixed = jnp.einsum('gts,bnsgw->bntgw', w_causal, v5) + b_s.T[:, :, None].astype(v.dtype)
    return u * mixed.reshape(B, S, GM_WIDTH)


def mla_branch(c_q, c_kv, k_pe, cos, sin, g_cq, w_uq, g_ckv, w_ukv,
               g_q_nope, g_q_pe, g_k_nope, g_k_pe):
    B, S, _ = c_q.shape
    q = (rmsnorm(c_q, g_cq) @ w_uq).reshape(B, S, MLA_HEADS, MLA_NOPE + MLA_ROPE)
    q_nope, q_pe = q[..., :MLA_NOPE], q[..., MLA_NOPE:]
    kv = (rmsnorm(c_kv, g_ckv) @ w_ukv).reshape(B, S, MLA_HEADS, MLA_NOPE + MLA_V)
    k_nope, v = kv[..., :MLA_NOPE], kv[..., MLA_NOPE:]
    q_nope = rmsnorm(q_nope, g_q_nope)
    k_nope = rmsnorm(k_nope, g_k_nope)
    q_pe = apply_rope(rmsnorm(q_pe, g_q_pe), cos[:, :, None, :], sin[:, :, None, :])
    k_pe = apply_rope(rmsnorm(k_pe, g_k_pe), cos, sin)
    scale = 1.0 / math.sqrt(MLA_NOPE + MLA_ROPE)
    nb = S // Q_BLOCK
    qn_b = q_nope.reshape(B, nb, Q_BLOCK, MLA_HEADS, MLA_NOPE).transpose(1, 0, 2, 3, 4)
    qp_b = q_pe.reshape(B, nb, Q_BLOCK, MLA_HEADS, MLA_ROPE).transpose(1, 0, 2, 3, 4)
    key_pos = jnp.arange(S)

    def block(args):
        qn, qp, i = args
        s = (jnp.einsum('bqhd,bkhd->bhqk', qn, k_nope)
             + jnp.einsum('bqhd,bkd->bhqk', qp, k_pe)).astype(jnp.float32) * scale
        q_pos = i * Q_BLOCK + jnp.arange(Q_BLOCK)
        s = jnp.where((q_pos[:, None] >= key_pos[None, :])[None, None], s, -jnp.inf)
        p = jax.nn.softmax(s, axis=-1).astype(v.dtype)
        return jnp.einsum('bhqk,bkhd->bqhd', p, v)

    out = lax.map(block, (qn_b, qp_b, jnp.arange(nb)))
    return out.transpose(1, 0, 2, 3, 4).reshape(B, S, MLA_HEADS * MLA_V)


def memory_branch(q_m, mem, g_mem, w_mem_kv, g_mq, g_mk):
    B, S, _ = q_m.shape
    M = mem.shape[1]
    q = rmsnorm(q_m.reshape(B, S, MEM_HEADS, MEM_HEAD_DIM), g_mq)
    kv = (rmsnorm(mem, g_mem) @ w_mem_kv).reshape(B, M, 2, MEM_HEADS, MEM_HEAD_DIM)
    k = rmsnorm(kv[:, :, 0], g_mk)
    v = kv[:, :, 1]
    s = jnp.einsum('bshd,bmhd->bhsm', q, k).astype(jnp.float32) / math.sqrt(MEM_HEAD_DIM)
    p = jax.nn.softmax(s, axis=-1).astype(v.dtype)
    return jnp.einsum('bhsm,bmhd->bshd', p, v).reshape(B, S, MEM_HEADS * MEM_HEAD_DIM)


def setup_inputs(seed: int = 0) -> dict:
    key = jax.random.key(seed)
    ks = iter(jax.random.split(key, 40))

    def w(shape, fan_in):
        return jax.random.normal(next(ks), (DEPTH,) + shape, jnp.float32) * (fan_in ** -0.5)

    def gain(n):
        return 1.0 + 0.02 * jax.random.normal(next(ks), (DEPTH, n), jnp.float32)

    x = jax.random.normal(next(ks), (BATCH, SEQ, D_MODEL), jnp.float32)
    mem = jax.random.normal(next(ks), (BATCH, MEM_LEN, D_MODEL), jnp.float32)
    offset = jax.random.randint(next(ks), (BATCH, 1), 0, 4096, dtype=jnp.int32)
    positions = (offset + jnp.arange(SEQ, dtype=jnp.int32)[None, :]).astype(jnp.int32)
    return {
        "x": x,
        "mem": mem,
        "positions": positions,
        "g_mix": gain(D_MODEL),
        "w_in": w((D_MODEL, W_IN_COLS), D_MODEL),
        "g_cq": gain(Q_LORA),
        "w_uq": w((Q_LORA, MLA_HEADS * (MLA_NOPE + MLA_ROPE)), Q_LORA),
        "g_ckv": gain(KV_LORA),
        "w_ukv": w((KV_LORA, MLA_HEADS * (MLA_NOPE + MLA_V)), KV_LORA),
        "g_q_nope": gain(MLA_NOPE),
        "g_q_pe": gain(MLA_ROPE),
        "g_k_nope": gain(MLA_NOPE),
        "g_k_pe": gain(MLA_ROPE),
        "g_gm_ln": gain(GM_WIDTH),
        "b_gm_ln": 0.02 * jax.random.normal(next(ks), (DEPTH, GM_WIDTH), jnp.float32),
        "w_spatial": w((GM_GROUPS, GM_CHUNK, GM_CHUNK), GM_CHUNK),
        "b_spatial": 1.0 + 0.02 * jax.random.normal(next(ks), (DEPTH, GM_GROUPS, GM_CHUNK), jnp.float32),
        "g_mem": gain(D_MODEL),
        "w_mem_kv": w((D_MODEL, 2 * MEM_HEADS * MEM_HEAD_DIM), D_MODEL),
        "g_mq": gain(MEM_HEAD_DIM),
        "g_mk": gain(MEM_HEAD_DIM),
        "w_o_gm": w((GM_WIDTH, D_MODEL), GM_WIDTH),
        "w_o_mla": w((MLA_HEADS * MLA_V, D_MODEL), MLA_HEADS * MLA_V),
        "w_o_mem": w((MEM_HEADS * MEM_HEAD_DIM, D_MODEL), MEM_HEADS * MEM_HEAD_DIM),
        "w_out": w((D_MODEL, D_MODEL), D_MODEL),
        "g_ffn": gain(D_MODEL),
        "w_ff1": w((D_MODEL, D_FF), D_MODEL),
        "w_ff2": w((D_FF, D_MODEL), D_FF),
    }


def reference(x, mem, positions, g_mix, w_in, g_cq, w_uq, g_ckv, w_ukv,
              g_q_nope, g_q_pe, g_k_nope, g_k_pe, g_gm_ln, b_gm_ln, w_spatial, b_spatial,
              g_mem, w_mem_kv, g_mq, g_mk, w_o_gm, w_o_mla, w_o_mem, w_out,
              g_ffn, w_ff1, w_ff2):
    cos, sin = rope_tables(positions)
    split_at = [GM_WIDTH, W_GM, W_GM + Q_LORA, W_GM + Q_LORA + KV_LORA,
                W_GM + W_MLA, W_GM + W_MLA + W_MEMQ]
    for l in range(DEPTH):
        h = rmsnorm(x, g_mix[l])
        z = h @ w_in[l]
        z_u, z_v, c_q, c_kv, k_pe, q_m, z_g = jnp.split(z, split_at, axis=-1)
        y_gm = gmlp_branch(z_u, z_v, g_gm_ln[l], b_gm_ln[l], w_spatial[l], b_spatial[l]) @ w_o_gm[l]
        y_mla = mla_branch(c_q, c_kv, k_pe, cos, sin, g_cq[l], w_uq[l], g_ckv[l], w_ukv[l],
                           g_q_nope[l], g_q_pe[l], g_k_nope[l], g_k_pe[l]) @ w_o_mla[l]
        y_mem = memory_branch(q_m, mem, g_mem[l], w_mem_kv[l], g_mq[l], g_mk[l]) @ w_o_mem[l]
        gates = jax.nn.sigmoid(z_g).reshape(z_g.shape[:-1] + (N_BRANCH, D_MODEL))
        merged = gates[..., 0, :] * y_gm + gates[..., 1, :] * y_mla + gates[..., 2, :] * y_mem
        x = x + merged @ w_out[l]
        h2 = rmsnorm(x, g_ffn[l])
        x = x + jnp.square(jax.nn.relu(h2 @ w_ff1[l])) @ w_ff2[l]
    return x
```

```python
import functools
import math

import jax
import jax.numpy as jnp
from jax import lax
from jax.experimental import pallas as pl
from jax.experimental.pallas import tpu as pltpu

F32 = jnp.float32
BF16 = jnp.bfloat16

EPS = 1e-6
ROPE_BASE = 10000.0

MEM_HEADS = 4
MEM_HEAD_DIM = 128
GM_CHUNK = 128
GM_GROUPS = 4
MLA_HEADS = 8
MLA_NOPE = 128
MLA_ROPE = 64
MLA_V = 128
Q_LORA = 384
KV_LORA = 256
N_BRANCH = 3

LANE = 128
V7X_VMEM_LIMIT_BYTES = 56 * 1024 * 1024

ROPE_PAD = LANE
QK_WIDTH = MLA_NOPE + ROPE_PAD

TOKEN_TILE = 512
MEM_TILE = 1024
ATTN_BLOCK = 256
FF_CHUNK = 1024


def _rms(x, g, n=None):
    n = x.shape[-1] if n is None else n
    ms = jnp.sum(x * x, axis=-1, keepdims=True) * (1.0 / n)
    return x * lax.rsqrt(ms + EPS) * g


def _gelu_tanh(x):
    c = math.sqrt(2.0 / math.pi)
    cdf = 0.5 * (1.0 + jnp.tanh(c * (x + 0.044715 * (x * x * x))))
    return x * cdf


def _dot(a, b):
    return jnp.dot(a, b, preferred_element_type=F32)


def _dot_nt(a, b):
    return lax.dot_general(a, b, (((1,), (1,)), ((), ())), preferred_element_type=F32)


def _swap_rope_halves(x):
    lane = lax.broadcasted_iota(jnp.int32, x.shape, 1)
    first_half = (lane % MLA_ROPE) < (MLA_ROPE // 2)
    return jnp.where(first_half,
                     pltpu.roll(x, LANE - MLA_ROPE // 2, axis=1),
                     pltpu.roll(x, MLA_ROPE // 2, axis=1))


def _memkv_kernel(mem_ref, g_mem_ref, w_ref, g_mk_ref, k_ref, v_ref):
    h = _rms(mem_ref[...], g_mem_ref[...]).astype(BF16)
    kv = _dot(h, w_ref[...])
    kw = MEM_HEADS * MEM_HEAD_DIM
    for hd in range(MEM_HEADS):
        sl = slice(hd * MEM_HEAD_DIM, (hd + 1) * MEM_HEAD_DIM)
        k_ref[:, sl] = _rms(kv[:, sl], g_mk_ref[...]).astype(BF16)
    v_ref[...] = kv[:, kw:].astype(BF16)


def _mixer_kernel(x_ref, pos_ref, invf_ref, g_mix_ref,
                  w_uv_ref, w_mla_ref, w_qm_ref, w_gate_ref,
                  g_ln_ref, b_ln_ref, ws_ref, bs_ref, w_ogm_ref,
                  g_cq_ref, wq_ref, g_ckv_ref, wkv_ref,
                  g_qn_ref, g_qp_ref, g_kn_ref, g_kp_ref,
                  kmem_ref, vmem_ref, g_mq_ref, w_omem_ref,
                  q_ref, k_ref, v_ref, mpart_ref, g1_ref,
                  mix_scr, att_scr):
    tm = x_ref.shape[0]
    gm_w = g_ln_ref.shape[-1]
    d_model = x_ref.shape[-1]

    h = _rms(x_ref[...], g_mix_ref[...]).astype(BF16)

    z_uv = _dot(h, w_uv_ref[...])
    u = _gelu_tanh(z_uv[:, :gm_w])
    gv = _gelu_tanh(z_uv[:, gm_w:])
    mu = jnp.mean(gv, axis=-1, keepdims=True)
    gc = gv - mu
    var = jnp.mean(gc * gc, axis=-1, keepdims=True)
    v_ln = (gc * lax.rsqrt(var + EPS) * g_ln_ref[...] + b_ln_ref[...]).astype(BF16)
    row = lax.broadcasted_iota(jnp.int32, (GM_CHUNK, GM_CHUNK), 0)
    col = lax.broadcasted_iota(jnp.int32, (GM_CHUNK, GM_CHUNK), 1)
    gw = gm_w // GM_GROUPS
    for g in range(GM_GROUPS):
        w_causal = jnp.where(row >= col, ws_ref[g], 0.0).astype(BF16)
        bias = bs_ref[:, g:g + 1]
        for c in range(tm // GM_CHUNK):
            rs = slice(c * GM_CHUNK, (c + 1) * GM_CHUNK)
            cs = slice(g * gw, (g + 1) * gw)
            mix_scr[rs, cs] = _dot(w_causal, v_ln[rs, cs]) + bias
    y_gm = _dot((u * mix_scr[...]).astype(BF16), w_ogm_ref[...])

    z_qm = _dot(h, w_qm_ref[...])
    mem_scale = 1.0 / math.sqrt(MEM_HEAD_DIM)
    for hd in range(MEM_HEADS):
        sl = slice(hd * MEM_HEAD_DIM, (hd + 1) * MEM_HEAD_DIM)
        qh = (_rms(z_qm[:, sl], g_mq_ref[...]) * mem_scale).astype(BF16)
        s = _dot_nt(qh, kmem_ref[:, sl])
        p = jnp.exp(s - jnp.max(s, axis=-1, keepdims=True))
        l = jnp.sum(p, axis=-1, keepdims=True)
        att_scr[:, sl] = _dot(p.astype(BF16), vmem_ref[:, sl]) / l
    y_mem = _dot(att_scr[...].astype(BF16), w_omem_ref[...])

    g0 = jax.nn.sigmoid(_dot(h, w_gate_ref[:, :d_model]))
    mpart = g0 * y_gm
    g2 = jax.nn.sigmoid(_dot(h, w_gate_ref[:, 2 * d_model:]))
    mpart_ref[...] = (mpart + g2 * y_mem).astype(BF16)
    g1_ref[...] = jax.nn.sigmoid(_dot(h, w_gate_ref[:, d_model:2 * d_model])).astype(BF16)

    half = MLA_ROPE // 2
    ang = invf_ref[...] * pos_ref[...].astype(F32)
    cos_t = jnp.cos(ang)
    sin_t = jnp.sin(ang)
    reps = LANE // MLA_ROPE
    cos_tab = jnp.concatenate([cos_t, cos_t] * reps, axis=0).T
    sin_tab = jnp.concatenate([-sin_t, sin_t] * reps, axis=0).T

    z_mla = _dot(h, w_mla_ref[...])
    cq = _rms(z_mla[:, :Q_LORA], g_cq_ref[...]).astype(BF16)
    ckv = _rms(z_mla[:, Q_LORA:Q_LORA + KV_LORA], g_ckv_ref[...]).astype(BF16)
    kpe = _rms(z_mla[:, Q_LORA + KV_LORA:], g_kp_ref[...], n=MLA_ROPE)
    kpe = (kpe * cos_tab + _swap_rope_halves(kpe) * sin_tab).astype(BF16)

    qk_scale = 1.0 / math.sqrt(MLA_NOPE + MLA_ROPE)
    q_all = _dot(cq, wq_ref[...])
    kv_all = _dot(ckv, wkv_ref[...])
    nope_w = MLA_HEADS * MLA_NOPE
    for hd in range(MLA_HEADS):
        ns = slice(hd * MLA_NOPE, (hd + 1) * MLA_NOPE)
        ps = slice(nope_w + hd * ROPE_PAD, nope_w + (hd + 1) * ROPE_PAD)
        qn = _rms(q_all[:, ns], g_qn_ref[...]) * qk_scale
        qp = _rms(q_all[:, ps], g_qp_ref[...], n=MLA_ROPE)
        qp = (qp * cos_tab + _swap_rope_halves(qp) * sin_tab) * qk_scale
        q_ref[hd, :, :MLA_NOPE] = qn.astype(BF16)
        q_ref[hd, :, MLA_NOPE:] = qp.astype(BF16)
        k_ref[hd, :, :MLA_NOPE] = _rms(kv_all[:, ns], g_kn_ref[...]).astype(BF16)
        k_ref[hd, :, MLA_NOPE:] = kpe
    v_ref[...] = kv_all[:, nope_w:].astype(BF16)


def _attn_kernel(q_ref, k_ref, v_ref, o_ref):
    blk = q_ref.shape[0]
    qi = pl.program_id(2)
    q = q_ref[...]

    def step(j, carry, masked):
        m, l, acc = carry
        start = pl.multiple_of(j * blk, blk)
        s = _dot_nt(q, k_ref[pl.ds(start, blk), :])
        if masked:
            r = lax.broadcasted_iota(jnp.int32, s.shape, 0)
            c = lax.broadcasted_iota(jnp.int32, s.shape, 1)
            s = jnp.where(r >= c, s, -jnp.inf)
        m_new = jnp.maximum(m, jnp.max(s, axis=-1, keepdims=True))
        alpha = jnp.exp(m - m_new)
        p = jnp.exp(s - m_new)
        l = alpha * l + jnp.sum(p, axis=-1, keepdims=True)
        acc = alpha * acc + _dot(p.astype(BF16), v_ref[pl.ds(start, blk), :])
        return m_new, l, acc

    init = (jnp.full((blk, 1), -1e30, F32), jnp.zeros((blk, 1), F32),
            jnp.zeros((blk, o_ref.shape[-1]), F32))
    carry = lax.fori_loop(0, qi, functools.partial(step, masked=False), init)
    _, l, acc = step(qi, carry, masked=True)
    o_ref[...] = (acc / l).astype(o_ref.dtype)


def _ffn_kernel(x_ref, a_ref, mpart_ref, g1_ref, w_omla_ref, w_out_ref, g_ffn_ref,
                w1_ref, w2_ref, o_ref):
    y_mla = _dot(a_ref[...], w_omla_ref[...])
    merged = mpart_ref[...].astype(F32) + g1_ref[...].astype(F32) * y_mla
    x1 = x_ref[...] + _dot(merged.astype(BF16), w_out_ref[...])
    h2 = _rms(x1, g_ffn_ref[...]).astype(BF16)
    acc = x1
    for c in range(w1_ref.shape[1] // FF_CHUNK):
        cs = slice(c * FF_CHUNK, (c + 1) * FF_CHUNK)
        t = jnp.maximum(_dot(h2, w1_ref[:, cs]), 0.0)
        acc = acc + _dot((t * t).astype(BF16), w2_ref[cs, :])
    o_ref[...] = acc


def _const_spec(shape):
    nd = len(shape)
    return pl.BlockSpec(shape, lambda *_: (0,) * nd, pipeline_mode=pl.Buffered(1))


def _params(sem):
    return pltpu.CompilerParams(dimension_semantics=sem,
                                vmem_limit_bytes=V7X_VMEM_LIMIT_BYTES)


def kernel(x, mem, positions, g_mix, w_in, g_cq, w_uq, g_ckv, w_ukv, g_q_nope, g_q_pe,
           g_k_nope, g_k_pe, g_gm_ln, b_gm_ln, w_spatial, b_spatial, g_mem, w_mem_kv,
           g_mq, g_mk, w_o_gm, w_o_mla, w_o_mem, w_out, g_ffn, w_ff1, w_ff2):
    B, S, D = x.shape
    M = mem.shape[1]
    T = B * S
    depth = g_mix.shape[0]
    gm_w = g_gm_ln.shape[-1]
    mem_w = MEM_HEADS * MEM_HEAD_DIM
    tm = TOKEN_TILE
    tiles_per_seq = S // tm
    assert S % tm == 0 and S % ATTN_BLOCK == 0 and (B * M) % MEM_TILE == 0

    inv_freq = (ROPE_BASE ** (-jnp.arange(0, MLA_ROPE, 2, dtype=F32) / MLA_ROPE)).reshape(-1, 1)
    pos = positions.reshape(T // tm, 1, tm)
    row = lambda a: a.reshape(1, -1)
    pad_rope = lambda g: jnp.pad(g, (0, ROPE_PAD - MLA_ROPE)).reshape(1, -1)

    xt = x.reshape(T, D)
    for l in range(depth):
        wi = w_in[l]
        c0 = 2 * gm_w
        c1 = c0 + Q_LORA + KV_LORA + MLA_ROPE
        c2 = c1 + mem_w
        w_uv = wi[:, :c0].astype(BF16)
        w_mla = jnp.pad(wi[:, c0:c1], ((0, 0), (0, ROPE_PAD - MLA_ROPE))).astype(BF16)
        w_qm = wi[:, c1:c2].astype(BF16)
        w_gate = wi[:, c2:].astype(BF16)
        wq3 = w_uq[l].reshape(Q_LORA, MLA_HEADS, MLA_NOPE + MLA_ROPE)
        wq_pe = jnp.pad(wq3[:, :, MLA_NOPE:], ((0, 0), (0, 0), (0, ROPE_PAD - MLA_ROPE)))
        wq = jnp.concatenate([wq3[:, :, :MLA_NOPE].reshape(Q_LORA, -1),
                              wq_pe.reshape(Q_LORA, -1)], axis=1).astype(BF16)
        wkv3 = w_ukv[l].reshape(KV_LORA, MLA_HEADS, MLA_NOPE + MLA_V)
        wkv = jnp.concatenate([wkv3[:, :, :MLA_NOPE].reshape(KV_LORA, -1),
                               wkv3[:, :, MLA_NOPE:].reshape(KV_LORA, -1)], axis=1).astype(BF16)

        k_mem, v_mem = pl.pallas_call(
            _memkv_kernel,
            grid=(B * M // MEM_TILE,),
            in_specs=[pl.BlockSpec((MEM_TILE, D), lambda i: (i, 0)),
                      _const_spec((1, D)), _const_spec((D, 2 * mem_w)),
                      _const_spec((1, MEM_HEAD_DIM))],
            out_specs=[pl.BlockSpec((MEM_TILE, mem_w), lambda i: (i, 0))] * 2,
            out_shape=[jax.ShapeDtypeStruct((B * M, mem_w), BF16)] * 2,
            compiler_params=_params(("parallel",)),
            name="memkv",
        )(mem.reshape(B * M, D), row(g_mem[l]), w_mem_kv[l].astype(BF16), row(g_mk[l]))
        k_mem = k_mem.reshape(B, M, mem_w)
        v_mem = v_mem.reshape(B, M, mem_w)

        tile_map = lambda i: (i, 0)
        head_map = lambda i: (i // tiles_per_seq, 0, i % tiles_per_seq, 0)
        mem_map = lambda i: (i // tiles_per_seq, 0, 0)
        consts = [
            inv_freq, row(g_mix[l]), w_uv, w_mla, w_qm, w_gate,
            row(g_gm_ln[l]), row(b_gm_ln[l]), w_spatial[l], b_spatial[l].T,
            w_o_gm[l].astype(BF16),
            row(g_cq[l]), wq, row(g_ckv[l]), wkv,
            row(g_q_nope[l]), pad_rope(g_q_pe[l]), row(g_k_nope[l]), pad_rope(g_k_pe[l]),
        ]
        tail_consts = [row(g_mq[l]), w_o_mem[l].astype(BF16)]
        q, k, v, mpart, g1 = pl.pallas_call(
            _mixer_kernel,
            grid=(T // tm,),
            in_specs=([pl.BlockSpec((tm, D), tile_map),
                       pl.BlockSpec((None, 1, tm), lambda i: (i, 0, 0))]
                      + [_const_spec(c.shape) for c in consts]
                      + [pl.BlockSpec((None, M, mem_w), mem_map)] * 2
                      + [_const_spec(c.shape) for c in tail_consts]),
            out_specs=[pl.BlockSpec((None, MLA_HEADS, tm, QK_WIDTH), head_map),
                       pl.BlockSpec((None, MLA_HEADS, tm, QK_WIDTH), head_map),
                       pl.BlockSpec((tm, MLA_HEADS * MLA_V), tile_map),
                       pl.BlockSpec((tm, D), tile_map),
                       pl.BlockSpec((tm, D), tile_map)],
            out_shape=[jax.ShapeDtypeStruct((B, MLA_HEADS, S, QK_WIDTH), BF16),
                       jax.ShapeDtypeStruct((B, MLA_HEADS, S, QK_WIDTH), BF16),
                       jax.ShapeDtypeStruct((T, MLA_HEADS * MLA_V), BF16),
                       jax.ShapeDtypeStruct((T, D), BF16),
                       jax.ShapeDtypeStruct((T, D), BF16)],
            scratch_shapes=[pltpu.VMEM((tm, gm_w), F32), pltpu.VMEM((tm, mem_w), F32)],
            compiler_params=_params(("parallel",)),
            name="mixer",
        )(xt, pos, *consts, k_mem, v_mem, *tail_consts)

        blk = ATTN_BLOCK
        attn = pl.pallas_call(
            _attn_kernel,
            grid=(B, MLA_HEADS, S // blk),
            in_specs=[pl.BlockSpec((None, None, blk, QK_WIDTH), lambda b, h, i: (b, h, i, 0)),
                      pl.BlockSpec((None, None, S, QK_WIDTH), lambda b, h, i: (b, h, 0, 0)),
                      pl.BlockSpec((None, S, MLA_V), lambda b, h, i: (b, 0, h))],
            out_specs=pl.BlockSpec((None, blk, MLA_V), lambda b, h, i: (b, i, h)),
            out_shape=jax.ShapeDtypeStruct((B, S, MLA_HEADS * MLA_V), BF16),
            compiler_params=_params(("parallel", "parallel", "arbitrary")),
            name="attn",
        )(q, k, v.reshape(B, S, MLA_HEADS * MLA_V))

        ffn_consts = [w_o_mla[l].astype(BF16), w_out[l].astype(BF16), row(g_ffn[l]),
                      w_ff1[l].astype(BF16), w_ff2[l].astype(BF16)]
        xt = pl.pallas_call(
            _ffn_kernel,
            grid=(T // tm,),
            in_specs=([pl.BlockSpec((tm, D), tile_map)] * 4
                      + [_const_spec(c.shape) for c in ffn_consts]),
            out_specs=pl.BlockSpec((tm, D), tile_map),
            out_shape=jax.ShapeDtypeStruct((T, D), F32),
            compiler_params=_params(("parallel",)),
            name="ffn",
        )(xt, attn.reshape(T, MLA_HEADS * MLA_V), mpart, g1, *ffn_consts)
    return xt.reshape(B, S, D)
```

```python
import functools
import math

import jax
import jax.numpy as jnp
from jax import lax
from jax.experimental import pallas as pl
from jax.experimental.pallas import tpu as pltpu

F32 = jnp.float32
BF16 = jnp.bfloat16

EPS = 1e-6
ROPE_BASE = 10000.0

MEM_HEADS = 4
MEM_HEAD_DIM = 128
GM_CHUNK = 128
GM_GROUPS = 4
MLA_HEADS = 8
MLA_NOPE = 128
MLA_ROPE = 64
MLA_V = 128
Q_LORA = 384
KV_LORA = 256
N_BRANCH = 3

LANE = 128
V7X_VMEM_LIMIT_BYTES = 56 * 1024 * 1024

ROPE_PAD = LANE
QK_WIDTH = MLA_NOPE + ROPE_PAD

TOKEN_TILE = 512
MEM_TILE = 1024
ATTN_BLOCK = 256
FF_CHUNK = 1024


def _rms(x, g, n=None):
    n = x.shape[-1] if n is None else n
    ms = jnp.sum(x * x, axis=-1, keepdims=True) * (1.0 / n)
    return x * lax.rsqrt(ms + EPS) * g


def _gelu_tanh(x):
    c = math.sqrt(2.0 / math.pi)
    cdf = 0.5 * (1.0 + jnp.tanh(c * (x + 0.044715 * (x * x * x))))
    return x * cdf


def _dot(a, b):
    return jnp.dot(a, b, preferred_element_type=F32)


def _dot_nt(a, b):
    return lax.dot_general(a, b, (((1,), (1,)), ((), ())), preferred_element_type=F32)


def _swap_rope_halves(x):
    lane = lax.broadcasted_iota(jnp.int32, x.shape, 1)
    first_half = (lane % MLA_ROPE) < (MLA_ROPE // 2)
    return jnp.where(first_half,
                     pltpu.roll(x, LANE - MLA_ROPE // 2, axis=1),
                     pltpu.roll(x, MLA_ROPE // 2, axis=1))


def _memkv_kernel(mem_ref, g_mem_ref, w_ref, g_mk_ref, k_ref, v_ref):
    h = _rms(mem_ref[...], g_mem_ref[...]).astype(BF16)
    kv = _dot(h, w_ref[...])
    kw = MEM_HEADS * MEM_HEAD_DIM
    for hd in range(MEM_HEADS):
        sl = slice(hd * MEM_HEAD_DIM, (hd + 1) * MEM_HEAD_DIM)
        k_ref[:, sl] = _rms(kv[:, sl], g_mk_ref[...]).astype(BF16)
    v_ref[...] = kv[:, kw:].astype(BF16)


def _mixer_kernel(x_ref, pos_ref, invf_ref, g_mix_ref,
                  w_uv_ref, w_mla_ref, w_qm_ref, w_gate_ref,
                  g_ln_ref, b_ln_ref, ws_ref, bs_ref, w_ogm_ref,
                  g_cq_ref, wq_ref, g_ckv_ref, wkv_ref,
                  g_qn_ref, g_qp_ref, g_kn_ref, g_kp_ref,
                  kmem_ref, vmem_ref, g_mq_ref, w_omem_ref,
                  q_ref, k_ref, v_ref, mpart_ref, g1_ref,
                  mix_scr, att_scr):
    tm = x_ref.shape[0]
    gm_w = g_ln_ref.shape[-1]
    d_model = x_ref.shape[-1]

    h = _rms(x_ref[...], g_mix_ref[...]).astype(BF16)

    z_uv = _dot(h, w_uv_ref[...])
    u = _gelu_tanh(z_uv[:, :gm_w])
    gv = _gelu_tanh(z_uv[:, gm_w:])
    mu = jnp.mean(gv, axis=-1, keepdims=True)
    gc = gv - mu
    var = jnp.mean(gc * gc, axis=-1, keepdims=True)
    v_ln = (gc * lax.rsqrt(var + EPS) * g_ln_ref[...] + b_ln_ref[...]).astype(BF16)
    row = lax.broadcasted_iota(jnp.int32, (GM_CHUNK, GM_CHUNK), 0)
    col = lax.broadcasted_iota(jnp.int32, (GM_CHUNK, GM_CHUNK), 1)
    gw = gm_w // GM_GROUPS
    for g in range(GM_GROUPS):
        w_causal = jnp.where(row >= col, ws_ref[g], 0.0).astype(BF16)
        bias = bs_ref[:, g:g + 1]
        for c in range(tm // GM_CHUNK):
            rs = slice(c * GM_CHUNK, (c + 1) * GM_CHUNK)
            cs = slice(g * gw, (g + 1) * gw)
            mix_scr[rs, cs] = _dot(w_causal, v_ln[rs, cs]) + bias
    y_gm = _dot((u * mix_scr[...]).astype(BF16), w_ogm_ref[...])

    z_qm = _dot(h, w_qm_ref[...])
    mem_scale = 1.0 / math.sqrt(MEM_HEAD_DIM)
    for hd in range(MEM_HEADS):
        sl = slice(hd * MEM_HEAD_DIM, (hd + 1) * MEM_HEAD_DIM)
        qh = (_rms(z_qm[:, sl], g_mq_ref[...]) * mem_scale).astype(BF16)
        s = _dot_nt(qh, kmem_ref[:, sl])
        p = jnp.exp(s - jnp.max(s, axis=-1, keepdims=True))
        l = jnp.sum(p, axis=-1, keepdims=True)
        att_scr[:, sl] = _dot(p.astype(BF16), vmem_ref[:, sl]) / l
    y_mem = _dot(att_scr[...].astype(BF16), w_omem_ref[...])

    g0 = jax.nn.sigmoid(_dot(h, w_gate_ref[:, :d_model]))
    mpart = g0 * y_gm
    g2 = jax.nn.sigmoid(_dot(h, w_gate_ref[:, 2 * d_model:]))
    mpart_ref[...] = (mpart + g2 * y_mem).astype(BF16)
    g1_ref[...] = jax.nn.sigmoid(_dot(h, w_gate_ref[:, d_model:2 * d_model])).astype(BF16)

    half = MLA_ROPE // 2
    ang = invf_ref[...] * pos_ref[...].astype(F32)
    cos_t = jnp.cos(ang)
    sin_t = jnp.sin(ang)
    reps = LANE // MLA_ROPE
    cos_tab = jnp.concatenate([cos_t, cos_t] * reps, axis=0).T
    sin_tab = jnp.concatenate([-sin_t, sin_t] * reps, axis=0).T

    z_mla = _dot(h, w_mla_ref[...])
    cq = _rms(z_mla[:, :Q_LORA], g_cq_ref[...]).astype(BF16)
    ckv = _rms(z_mla[:, Q_LORA:Q_LORA + KV_LORA], g_ckv_ref[...]).astype(BF16)
    kpe = _rms(z_mla[:, Q_LORA + KV_LORA:], g_kp_ref[...], n=MLA_ROPE)
    kpe = (kpe * cos_tab + _swap_rope_halves(kpe) * sin_tab).astype(BF16)

    qk_scale = math.log2(math.e) / math.sqrt(MLA_NOPE + MLA_ROPE)
    q_all = _dot(cq, wq_ref[...])
    kv_all = _dot(ckv, wkv_ref[...])
    nope_w = MLA_HEADS * MLA_NOPE
    for hd in range(MLA_HEADS):
        ns = slice(hd * MLA_NOPE, (hd + 1) * MLA_NOPE)
        ps = slice(nope_w + hd * ROPE_PAD, nope_w + (hd + 1) * ROPE_PAD)
        qn = _rms(q_all[:, ns], g_qn_ref[...]) * qk_scale
        qp = _rms(q_all[:, ps], g_qp_ref[...], n=MLA_ROPE)
        qp = (qp * cos_tab + _swap_rope_halves(qp) * sin_tab) * qk_scale
        q_ref[hd, :, :MLA_NOPE] = qn.astype(BF16)
        q_ref[hd, :, MLA_NOPE:] = qp.astype(BF16)
        k_ref[hd, :, :MLA_NOPE] = _rms(kv_all[:, ns], g_kn_ref[...]).astype(BF16)
        k_ref[hd, :, MLA_NOPE:] = kpe
    v_ref[...] = kv_all[:, nope_w:].astype(BF16)


def _attn_kernel(q_ref, k_ref, v_ref, o_ref, vext_scr):
    seq, vw = v_ref.shape
    blk = ATTN_BLOCK
    vext_scr[:, :vw] = v_ref[...]
    vext_scr[:, vw:] = jnp.ones((seq, vext_scr.shape[1] - vw), BF16)
    r = lax.broadcasted_iota(jnp.int32, (blk, blk), 0)
    c = lax.broadcasted_iota(jnp.int32, (blk, blk), 1)
    causal = r >= c
    for i in range(seq // blk):
        lo = i * blk
        q = q_ref[lo:lo + blk, :]
        s_d = jnp.where(causal, _dot_nt(q, k_ref[lo:lo + blk, :]), -1e30)
        m = jnp.max(s_d, axis=-1, keepdims=True)
        if i > 0:
            s_f = _dot_nt(q, k_ref[:lo, :])
            m = jnp.maximum(m, jnp.max(s_f, axis=-1, keepdims=True))
            o = _dot(jnp.exp2(s_f - m).astype(BF16), vext_scr[:lo, :])
            o = o + _dot(jnp.exp2(s_d - m).astype(BF16), vext_scr[lo:lo + blk, :])
        else:
            o = _dot(jnp.exp2(s_d - m).astype(BF16), vext_scr[lo:lo + blk, :])
        o_ref[lo:lo + blk, :] = (o[:, :vw] / o[:, vw:]).astype(o_ref.dtype)


def _ffn_kernel(x_ref, a_ref, mpart_ref, g1_ref, w_omla_ref, w_out_ref, g_ffn_ref,
                w1_ref, w2_ref, o_ref):
    y_mla = _dot(a_ref[...], w_omla_ref[...])
    merged = mpart_ref[...].astype(F32) + g1_ref[...].astype(F32) * y_mla
    x1 = x_ref[...] + _dot(merged.astype(BF16), w_out_ref[...])
    h2 = _rms(x1, g_ffn_ref[...]).astype(BF16)
    acc = x1
    for c in range(w1_ref.shape[1] // FF_CHUNK):
        cs = slice(c * FF_CHUNK, (c + 1) * FF_CHUNK)
        t = jnp.maximum(_dot(h2, w1_ref[:, cs]), 0.0)
        acc = acc + _dot((t * t).astype(BF16), w2_ref[cs, :])
    o_ref[...] = acc


def _const_spec(shape):
    nd = len(shape)
    return pl.BlockSpec(shape, lambda *_: (0,) * nd, pipeline_mode=pl.Buffered(1))


def _params(sem):
    return pltpu.CompilerParams(dimension_semantics=sem,
                                vmem_limit_bytes=V7X_VMEM_LIMIT_BYTES)


def kernel(x, mem, positions, g_mix, w_in, g_cq, w_uq, g_ckv, w_ukv, g_q_nope, g_q_pe,
           g_k_nope, g_k_pe, g_gm_ln, b_gm_ln, w_spatial, b_spatial, g_mem, w_mem_kv,
           g_mq, g_mk, w_o_gm, w_o_mla, w_o_mem, w_out, g_ffn, w_ff1, w_ff2):
    B, S, D = x.shape
    M = mem.shape[1]
    T = B * S
    depth = g_mix.shape[0]
    gm_w = g_gm_ln.shape[-1]
    mem_w = MEM_HEADS * MEM_HEAD_DIM
    tm = TOKEN_TILE
    tiles_per_seq = S // tm
    assert S % tm == 0 and S % ATTN_BLOCK == 0 and (B * M) % MEM_TILE == 0

    inv_freq = (ROPE_BASE ** (-jnp.arange(0, MLA_ROPE, 2, dtype=F32) / MLA_ROPE)).reshape(-1, 1)
    pos = positions.reshape(T // tm, 1, tm)
    row = lambda a: a.reshape(1, -1)
    pad_rope = lambda g: jnp.pad(g, (0, ROPE_PAD - MLA_ROPE)).reshape(1, -1)

    xt = x.reshape(T, D)
    for l in range(depth):
        wi = w_in[l]
        c0 = 2 * gm_w
        c1 = c0 + Q_LORA + KV_LORA + MLA_ROPE
        c2 = c1 + mem_w
        w_uv = wi[:, :c0].astype(BF16)
        w_mla = jnp.pad(wi[:, c0:c1], ((0, 0), (0, ROPE_PAD - MLA_ROPE))).astype(BF16)
        w_qm = wi[:, c1:c2].astype(BF16)
        w_gate = wi[:, c2:].astype(BF16)
        wq3 = w_uq[l].reshape(Q_LORA, MLA_HEADS, MLA_NOPE + MLA_ROPE)
        wq_pe = jnp.pad(wq3[:, :, MLA_NOPE:], ((0, 0), (0, 0), (0, ROPE_PAD - MLA_ROPE)))
        wq = jnp.concatenate([wq3[:, :, :MLA_NOPE].reshape(Q_LORA, -1),
                              wq_pe.reshape(Q_LORA, -1)], axis=1).astype(BF16)
        wkv3 = w_ukv[l].reshape(KV_LORA, MLA_HEADS, MLA_NOPE + MLA_V)
        wkv = jnp.concatenate([wkv3[:, :, :MLA_NOPE].reshape(KV_LORA, -1),
                               wkv3[:, :, MLA_NOPE:].reshape(KV_LORA, -1)], axis=1).astype(BF16)

        k_mem, v_mem = pl.pallas_call(
            _memkv_kernel,
            grid=(B * M // MEM_TILE,),
            in_specs=[pl.BlockSpec((MEM_TILE, D), lambda i: (i, 0)),
                      _const_spec((1, D)), _const_spec((D, 2 * mem_w)),
                      _const_spec((1, MEM_HEAD_DIM))],
            out_specs=[pl.BlockSpec((MEM_TILE, mem_w), lambda i: (i, 0))] * 2,
            out_shape=[jax.ShapeDtypeStruct((B * M, mem_w), BF16)] * 2,
            compiler_params=_params(("parallel",)),
            name="memkv",
        )(mem.reshape(B * M, D), row(g_mem[l]), w_mem_kv[l].astype(BF16), row(g_mk[l]))
        k_mem = k_mem.reshape(B, M, mem_w)
        v_mem = v_mem.reshape(B, M, mem_w)

        tile_map = lambda i: (i, 0)
        head_map = lambda i: (i // tiles_per_seq, 0, i % tiles_per_seq, 0)
        mem_map = lambda i: (i // tiles_per_seq, 0, 0)
        consts = [
            inv_freq, row(g_mix[l]), w_uv, w_mla, w_qm, w_gate,
            row(g_gm_ln[l]), row(b_gm_ln[l]), w_spatial[l], b_spatial[l].T,
            w_o_gm[l].astype(BF16),
            row(g_cq[l]), wq, row(g_ckv[l]), wkv,
            row(g_q_nope[l]), pad_rope(g_q_pe[l]), row(g_k_nope[l]), pad_rope(g_k_pe[l]),
        ]
        tail_consts = [row(g_mq[l]), w_o_mem[l].astype(BF16)]
        q, k, v, mpart, g1 = pl.pallas_call(
            _mixer_kernel,
            grid=(T // tm,),
            in_specs=([pl.BlockSpec((tm, D), tile_map),
                       pl.BlockSpec((None, 1, tm), lambda i: (i, 0, 0))]
                      + [_const_spec(c.shape) for c in consts]
                      + [pl.BlockSpec((None, M, mem_w), mem_map)] * 2
                      + [_const_spec(c.shape) for c in tail_consts]),
            out_specs=[pl.BlockSpec((None, MLA_HEADS, tm, QK_WIDTH), head_map),
                       pl.BlockSpec((None, MLA_HEADS, tm, QK_WIDTH), head_map),
                       pl.BlockSpec((tm, MLA_HEADS * MLA_V), tile_map),
                       pl.BlockSpec((tm, D), tile_map),
                       pl.BlockSpec((tm, D), tile_map)],
            out_shape=[jax.ShapeDtypeStruct((B, MLA_HEADS, S, QK_WIDTH), BF16),
                       jax.ShapeDtypeStruct((B, MLA_HEADS, S, QK_WIDTH), BF16),
                       jax.ShapeDtypeStruct((T, MLA_HEADS * MLA_V), BF16),
                       jax.ShapeDtypeStruct((T, D), BF16),
                       jax.ShapeDtypeStruct((T, D), BF16)],
            scratch_shapes=[pltpu.VMEM((tm, gm_w), F32), pltpu.VMEM((tm, mem_w), F32)],
            compiler_params=_params(("parallel",)),
            name="mixer",
        )(xt, pos, *consts, k_mem, v_mem, *tail_consts)

        attn = pl.pallas_call(
            _attn_kernel,
            grid=(B, MLA_HEADS),
            in_specs=[pl.BlockSpec((None, None, S, QK_WIDTH), lambda b, h: (b, h, 0, 0)),
                      pl.BlockSpec((None, None, S, QK_WIDTH), lambda b, h: (b, h, 0, 0)),
                      pl.BlockSpec((None, S, MLA_V), lambda b, h: (b, 0, h))],
            out_specs=pl.BlockSpec((None, S, MLA_V), lambda b, h: (b, 0, h)),
            out_shape=jax.ShapeDtypeStruct((B, S, MLA_HEADS * MLA_V), BF16),
            scratch_shapes=[pltpu.VMEM((S, 2 * MLA_V), BF16)],
            compiler_params=_params(("parallel", "parallel")),
            name="attn",
        )(q, k, v.reshape(B, S, MLA_HEADS * MLA_V))

        ffn_consts = [w_o_mla[l].astype(BF16), w_out[l].astype(BF16), row(g_ffn[l]),
                      w_ff1[l].astype(BF16), w_ff2[l].astype(BF16)]
        xt = pl.pallas_call(
            _ffn_kernel,
            grid=(T // tm,),
            in_specs=([pl.BlockSpec((tm, D), tile_map)] * 4
                      + [_const_spec(c.shape) for c in ffn_consts]),
            out_specs=pl.BlockSpec((tm, D), tile_map),
            out_shape=jax.ShapeDtypeStruct((T, D), F32),
            compiler_params=_params(("parallel",)),
            name="ffn",
        )(xt, attn.reshape(T, MLA_HEADS * MLA_V), mpart, g1, *ffn_consts)
    return xt.reshape(B, S, D)
```

```python
import math

import jax
import jax.numpy as jnp
from jax import lax
from jax.experimental import pallas as pl
from jax.experimental.pallas import tpu as pltpu

F32 = jnp.float32
BF16 = jnp.bfloat16

EPS = 1e-6
ROPE_BASE = 10000.0

MEM_HEADS = 4
MEM_HEAD_DIM = 128
GM_CHUNK = 128
GM_GROUPS = 4
MLA_HEADS = 8
MLA_NOPE = 128
MLA_ROPE = 64
MLA_V = 128
Q_LORA = 384
KV_LORA = 256
N_BRANCH = 3

LANE = 128
V7X_VMEM_LIMIT_BYTES = 56 * 1024 * 1024

ROPE_SLOT = 2 * MLA_ROPE
assert ROPE_SLOT == LANE
QK_WIDTH = MLA_NOPE + ROPE_SLOT

TOKEN_TILE = 512
MEM_TILE = 1024
ATTN_BLOCK = 512
FF_CHUNK = 1024


def _rms(x, g):
    ms = jnp.sum(x * x, axis=-1, keepdims=True) * (1.0 / x.shape[-1])
    return x * lax.rsqrt(ms + EPS) * g


def _gelu_tanh(x):
    c = math.sqrt(2.0 / math.pi)
    cdf = 0.5 * (1.0 + jnp.tanh(c * (x + 0.044715 * (x * x * x))))
    return x * cdf


def _dot(a, b):
    return jnp.dot(a, b, preferred_element_type=F32)


def _dot_nt(a, b):
    return lax.dot_general(a, b, (((1,), (1,)), ((), ())), preferred_element_type=F32)


def _swap_halves(a):
    half = a.shape[-1] // 2
    return jnp.concatenate([a[..., half:], a[..., :half]], axis=-1)


def _memkv_kernel(mem_ref, g_mem_ref, w_ref, g_mk_ref, k_ref, v_ref):
    h = _rms(mem_ref[...], g_mem_ref[...]).astype(BF16)
    kv = _dot(h, w_ref[...])
    kw = MEM_HEADS * MEM_HEAD_DIM
    for hd in range(MEM_HEADS):
        sl = slice(hd * MEM_HEAD_DIM, (hd + 1) * MEM_HEAD_DIM)
        k_ref[:, sl] = _rms(kv[:, sl], g_mk_ref[...]).astype(BF16)
    v_ref[...] = kv[:, kw:].astype(BF16)


def _mixer_kernel(x_ref, pos_ref, invf_ref, g_mix_ref,
                  w_uv_ref, w_mla_ref, w_qm_ref, w_gate_ref,
                  g_ln_ref, b_ln_ref, ws_ref, bs_ref, w_ogm_ref,
                  g_cq_ref, wq_ref, g_ckv_ref, wkv_ref,
                  g_qn_ref, g_qp_ref, g_kn_ref, g_kp_ref,
                  kmem_ref, vmem_ref, g_mq_ref, w_omem_ref,
                  q_ref, k_ref, v_ref, mpart_ref, g1_ref,
                  mix_scr, att_scr):
    tm = x_ref.shape[0]
    gm_w = g_ln_ref.shape[-1]
    d_model = x_ref.shape[-1]

    h = _rms(x_ref[...], g_mix_ref[...]).astype(BF16)

    ang = invf_ref[...] * pos_ref[...].astype(F32)
    cos_t = jnp.cos(ang)
    sin_t = jnp.sin(ang)
    rope_tab = jnp.concatenate([cos_t, cos_t, -sin_t, sin_t], axis=0).T

    z_mla = _dot(h, w_mla_ref[...])
    cq = _rms(z_mla[:, :Q_LORA], g_cq_ref[...]).astype(BF16)
    ckv = _rms(z_mla[:, Q_LORA:Q_LORA + KV_LORA], g_ckv_ref[...]).astype(BF16)
    kp = _rms(z_mla[:, Q_LORA + KV_LORA:], g_kp_ref[...]) * rope_tab
    kp = (kp + pltpu.roll(kp, MLA_ROPE, axis=1)).astype(BF16)

    qk_scale = math.log2(math.e) / math.sqrt(MLA_NOPE + MLA_ROPE)
    g_qn = g_qn_ref[...] * qk_scale
    q_tab = rope_tab * (g_qp_ref[...] * qk_scale)
    q_all = _dot(cq, wq_ref[...])
    kv_all = _dot(ckv, wkv_ref[...])
    nope_w = MLA_HEADS * MLA_NOPE
    for hd in range(MLA_HEADS):
        ns = slice(hd * MLA_NOPE, (hd + 1) * MLA_NOPE)
        ps = slice(nope_w + hd * ROPE_SLOT, nope_w + (hd + 1) * ROPE_SLOT)
        q_ref[hd, :, :MLA_NOPE] = _rms(q_all[:, ns], g_qn).astype(BF16)
        q_ref[hd, :, MLA_NOPE:] = _rms(q_all[:, ps], q_tab).astype(BF16)
        k_ref[hd, :, :MLA_NOPE] = _rms(kv_all[:, ns], g_kn_ref[...]).astype(BF16)
        k_ref[hd, :, MLA_NOPE:] = kp
    v_ref[...] = kv_all[:, nope_w:].astype(BF16)

    z_uv = _dot(h, w_uv_ref[...])
    u = _gelu_tanh(z_uv[:, :gm_w])
    gv = _gelu_tanh(z_uv[:, gm_w:])
    mu = jnp.mean(gv, axis=-1, keepdims=True)
    gc = gv - mu
    var = jnp.mean(gc * gc, axis=-1, keepdims=True)
    v_ln = (gc * lax.rsqrt(var + EPS) * g_ln_ref[...] + b_ln_ref[...]).astype(BF16)
    row = lax.broadcasted_iota(jnp.int32, (GM_CHUNK, GM_CHUNK), 0)
    col = lax.broadcasted_iota(jnp.int32, (GM_CHUNK, GM_CHUNK), 1)
    gw = gm_w // GM_GROUPS
    for g in range(GM_GROUPS):
        w_causal = jnp.where(row >= col, ws_ref[g], 0.0).astype(BF16)
        bias = bs_ref[:, g:g + 1]
        for c in range(tm // GM_CHUNK):
            rs = slice(c * GM_CHUNK, (c + 1) * GM_CHUNK)
            cs = slice(g * gw, (g + 1) * gw)
            mix_scr[rs, cs] = _dot(w_causal, v_ln[rs, cs]) + bias
    y_gm = _dot((u * mix_scr[...]).astype(BF16), w_ogm_ref[...])

    z_qm = _dot(h, w_qm_ref[...])
    g_mq = g_mq_ref[...] * (1.0 / math.sqrt(MEM_HEAD_DIM))
    for hd in range(MEM_HEADS):
        sl = slice(hd * MEM_HEAD_DIM, (hd + 1) * MEM_HEAD_DIM)
        qh = _rms(z_qm[:, sl], g_mq).astype(BF16)
        s = _dot_nt(qh, kmem_ref[:, sl])
        p = jnp.exp(s - jnp.max(s, axis=-1, keepdims=True))
        l = jnp.sum(p, axis=-1, keepdims=True)
        att_scr[:, sl] = _dot(p.astype(BF16), vmem_ref[:, sl]) / l
    y_mem = _dot(att_scr[...].astype(BF16), w_omem_ref[...])

    g1_ref[...] = jax.nn.sigmoid(_dot(h, w_gate_ref[:, d_model:2 * d_model])).astype(BF16)
    g0 = jax.nn.sigmoid(_dot(h, w_gate_ref[:, :d_model]))
    mpart = g0 * y_gm
    g2 = jax.nn.sigmoid(_dot(h, w_gate_ref[:, 2 * d_model:]))
    mpart_ref[...] = (mpart + g2 * y_mem).astype(BF16)


def _attn_kernel(q_ref, k_ref, v_ref, o_ref, vext_scr):
    seq, vw = v_ref.shape
    blk = ATTN_BLOCK
    vext_scr[:, :vw] = v_ref[...]
    vext_scr[:, vw:] = jnp.ones((seq, vext_scr.shape[1] - vw), BF16)
    r = lax.broadcasted_iota(jnp.int32, (blk, blk), 0)
    c = lax.broadcasted_iota(jnp.int32, (blk, blk), 1)
    causal = r >= c
    for i in reversed(range(seq // blk)):
        lo = i * blk
        q = q_ref[lo:lo + blk, :]
        s_d = jnp.where(causal, _dot_nt(q, k_ref[lo:lo + blk, :]), -1e30)
        m = jnp.max(s_d, axis=-1, keepdims=True)
        if i > 0:
            s_f = _dot_nt(q, k_ref[:lo, :])
            m = jnp.maximum(m, jnp.max(s_f, axis=-1, keepdims=True))
            o = _dot(jnp.exp2(s_f - m).astype(BF16), vext_scr[:lo, :])
            o = o + _dot(jnp.exp2(s_d - m).astype(BF16), vext_scr[lo:lo + blk, :])
        else:
            o = _dot(jnp.exp2(s_d - m).astype(BF16), vext_scr[lo:lo + blk, :])
        o_ref[lo:lo + blk, :] = (o[:, :vw] / o[:, vw:]).astype(o_ref.dtype)


def _ffn_kernel(x_ref, a_ref, mpart_ref, g1_ref, w_omla_ref, w_out_ref, g_ffn_ref,
                w1_ref, w2_ref, o_ref):
    y_mla = _dot(a_ref[...], w_omla_ref[...])
    merged = mpart_ref[...].astype(F32) + g1_ref[...].astype(F32) * y_mla
    x1 = x_ref[...] + _dot(merged.astype(BF16), w_out_ref[...])
    h2 = _rms(x1, g_ffn_ref[...]).astype(BF16)
    acc = x1
    for c in range(w1_ref.shape[1] // FF_CHUNK):
        cs = slice(c * FF_CHUNK, (c + 1) * FF_CHUNK)
        t = jnp.maximum(_dot(h2, w1_ref[:, cs]), 0.0)
        acc = acc + _dot((t * t).astype(BF16), w2_ref[cs, :])
    o_ref[...] = acc


def _const_spec(shape):
    nd = len(shape)
    return pl.BlockSpec(shape, lambda *_: (0,) * nd, pipeline_mode=pl.Buffered(1))


def _params(sem):
    return pltpu.CompilerParams(dimension_semantics=sem,
                                vmem_limit_bytes=V7X_VMEM_LIMIT_BYTES)


def kernel(x, mem, positions, g_mix, w_in, g_cq, w_uq, g_ckv, w_ukv, g_q_nope, g_q_pe,
           g_k_nope, g_k_pe, g_gm_ln, b_gm_ln, w_spatial, b_spatial, g_mem, w_mem_kv,
           g_mq, g_mk, w_o_gm, w_o_mla, w_o_mem, w_out, g_ffn, w_ff1, w_ff2):
    B, S, D = x.shape
    M = mem.shape[1]
    T = B * S
    depth = g_mix.shape[0]
    gm_w = g_gm_ln.shape[-1]
    mem_w = MEM_HEADS * MEM_HEAD_DIM
    tm = TOKEN_TILE
    tiles_per_seq = S // tm
    assert S % tm == 0 and S % ATTN_BLOCK == 0 and (B * M) % MEM_TILE == 0

    inv_freq = (ROPE_BASE ** (-jnp.arange(0, MLA_ROPE, 2, dtype=F32) / MLA_ROPE)).reshape(-1, 1)
    pos = positions.reshape(T // tm, 1, tm)
    row = lambda a: a.reshape(1, -1)
    with_swap = lambda a: jnp.concatenate([a, _swap_halves(a)], axis=-1)

    xt = x.reshape(T, D)
    for l in range(depth):
        wi = w_in[l]
        c0 = 2 * gm_w
        c1 = c0 + Q_LORA + KV_LORA + MLA_ROPE
        c2 = c1 + mem_w
        w_uv = wi[:, :c0].astype(BF16)
        w_mla = jnp.concatenate([wi[:, c0:c1], _swap_halves(wi[:, c1 - MLA_ROPE:c1])],
                                axis=1).astype(BF16)
        w_qm = wi[:, c1:c2].astype(BF16)
        w_gate = wi[:, c2:].astype(BF16)
        wq3 = w_uq[l].reshape(Q_LORA, MLA_HEADS, MLA_NOPE + MLA_ROPE)
        wq = jnp.concatenate([wq3[:, :, :MLA_NOPE].reshape(Q_LORA, -1),
                              with_swap(wq3[:, :, MLA_NOPE:]).reshape(Q_LORA, -1)],
                             axis=1).astype(BF16)
        wkv3 = w_ukv[l].reshape(KV_LORA, MLA_HEADS, MLA_NOPE + MLA_V)
        wkv = jnp.concatenate([wkv3[:, :, :MLA_NOPE].reshape(KV_LORA, -1),
                               wkv3[:, :, MLA_NOPE:].reshape(KV_LORA, -1)], axis=1).astype(BF16)

        k_mem, v_mem = pl.pallas_call(
            _memkv_kernel,
            grid=(B * M // MEM_TILE,),
            in_specs=[pl.BlockSpec((MEM_TILE, D), lambda i: (i, 0)),
                      _const_spec((1, D)), _const_spec((D, 2 * mem_w)),
                      _const_spec((1, MEM_HEAD_DIM))],
            out_specs=[pl.BlockSpec((MEM_TILE, mem_w), lambda i: (i, 0))] * 2,
            out_shape=[jax.ShapeDtypeStruct((B * M, mem_w), BF16)] * 2,
            compiler_params=_params(("parallel",)),
            name="memkv",
        )(mem.reshape(B * M, D), row(g_mem[l]), w_mem_kv[l].astype(BF16), row(g_mk[l]))
        k_mem = k_mem.reshape(B, M, mem_w)
        v_mem = v_mem.reshape(B, M, mem_w)

        tile_map = lambda i: (i, 0)
        head_map = lambda i: (i // tiles_per_seq, 0, i % tiles_per_seq, 0)
        mem_map = lambda i: (i // tiles_per_seq, 0, 0)
        consts = [
            inv_freq, row(g_mix[l]), w_uv, w_mla, w_qm, w_gate,
            row(g_gm_ln[l]), row(b_gm_ln[l]), w_spatial[l], b_spatial[l].T,
            w_o_gm[l].astype(BF16),
            row(g_cq[l]), wq, row(g_ckv[l]), wkv,
            row(g_q_nope[l]), row(with_swap(g_q_pe[l])),
            row(g_k_nope[l]), row(with_swap(g_k_pe[l])),
        ]
        tail_consts = [row(g_mq[l]), w_o_mem[l].astype(BF16)]
        q, k, v, mpart, g1 = pl.pallas_call(
            _mixer_kernel,
            grid=(T // tm,),
            in_specs=([pl.BlockSpec((tm, D), tile_map),
                       pl.BlockSpec((None, 1, tm), lambda i: (i, 0, 0))]
                      + [_const_spec(c.shape) for c in consts]
                      + [pl.BlockSpec((None, M, mem_w), mem_map)] * 2
                      + [_const_spec(c.shape) for c in tail_consts]),
            out_specs=[pl.BlockSpec((None, MLA_HEADS, tm, QK_WIDTH), head_map),
                       pl.BlockSpec((None, MLA_HEADS, tm, QK_WIDTH), head_map),
                       pl.BlockSpec((tm, MLA_HEADS * MLA_V), tile_map),
                       pl.BlockSpec((tm, D), tile_map),
                       pl.BlockSpec((tm, D), tile_map)],
            out_shape=[jax.ShapeDtypeStruct((B, MLA_HEADS, S, QK_WIDTH), BF16),
                       jax.ShapeDtypeStruct((B, MLA_HEADS, S, QK_WIDTH), BF16),
                       jax.ShapeDtypeStruct((T, MLA_HEADS * MLA_V), BF16),
                       jax.ShapeDtypeStruct((T, D), BF16),
                       jax.ShapeDtypeStruct((T, D), BF16)],
            scratch_shapes=[pltpu.VMEM((tm, gm_w), F32), pltpu.VMEM((tm, mem_w), F32)],
            compiler_params=_params(("parallel",)),
            name="mixer",
        )(xt, pos, *consts, k_mem, v_mem, *tail_consts)

        attn = pl.pallas_call(
            _attn_kernel,
            grid=(B, MLA_HEADS),
            in_specs=[pl.BlockSpec((None, None, S, QK_WIDTH), lambda b, h: (b, h, 0, 0)),
                      pl.BlockSpec((None, None, S, QK_WIDTH), lambda b, h: (b, h, 0, 0)),
                      pl.BlockSpec((None, S, MLA_V), lambda b, h: (b, 0, h))],
            out_specs=pl.BlockSpec((None, S, MLA_V), lambda b, h: (b, 0, h)),
            out_shape=jax.ShapeDtypeStruct((B, S, MLA_HEADS * MLA_V), BF16),
            scratch_shapes=[pltpu.VMEM((S, 2 * MLA_V), BF16)],
            compiler_params=_params(("parallel", "parallel")),
            name="attn",
        )(q, k, v.reshape(B, S, MLA_HEADS * MLA_V))

        ffn_consts = [w_o_mla[l].astype(BF16), w_out[l].astype(BF16), row(g_ffn[l]),
                      w_ff1[l].astype(BF16), w_ff2[l].astype(BF16)]
        xt = pl.pallas_call(
            _ffn_kernel,
            grid=(T // tm,),
            in_specs=([pl.BlockSpec((tm, D), tile_map)] * 4
                      + [_const_spec(c.shape) for c in ffn_consts]),
            out_specs=pl.BlockSpec((tm, D), tile_map),
            out_shape=jax.ShapeDtypeStruct((T, D), F32),
            compiler_params=_params(("parallel",)),
            name="ffn",
        )(xt, attn.reshape(T, MLA_HEADS * MLA_V), mpart, g1, *ffn_consts)
    return xt.reshape(B, S, D)
```

```python
import math

import jax
import jax.numpy as jnp
from jax import lax
from jax.experimental import pallas as pl
from jax.experimental.pallas import tpu as pltpu

F32 = jnp.float32
BF16 = jnp.bfloat16

EPS = 1e-6
ROPE_BASE = 10000.0

MEM_HEADS = 4
MEM_HEAD_DIM = 128
GM_CHUNK = 128
GM_GROUPS = 4
MLA_HEADS = 8
MLA_NOPE = 128
MLA_ROPE = 64
MLA_V = 128
Q_LORA = 384
KV_LORA = 256
N_BRANCH = 3

LANE = 128
V7X_VMEM_LIMIT_BYTES = 56 * 1024 * 1024

ROPE_SLOT = 2 * MLA_ROPE
assert ROPE_SLOT == LANE
QK_WIDTH = MLA_NOPE + ROPE_SLOT

TOKEN_TILE = 512
MEM_TILE = 1024
ATTN_BLOCK = 512
FF_CHUNK = 1024


def _rms(x, g):
    ms = jnp.sum(x * x, axis=-1, keepdims=True) * (1.0 / x.shape[-1])
    return x * lax.rsqrt(ms + EPS) * g


def _gelu_tanh(x):
    c = math.sqrt(2.0 / math.pi)
    cdf = 0.5 * (1.0 + jnp.tanh(c * (x + 0.044715 * (x * x * x))))
    return x * cdf


def _dot(a, b):
    return jnp.dot(a, b, preferred_element_type=F32)


def _dot_nt(a, b):
    return lax.dot_general(a, b, (((1,), (1,)), ((), ())), preferred_element_type=F32)


def _swap_halves(a):
    half = a.shape[-1] // 2
    return jnp.concatenate([a[..., half:], a[..., :half]], axis=-1)


def _memkv_kernel(mem_ref, g_mem_ref, w_ref, g_mk_ref, k_ref, v_ref):
    h = _rms(mem_ref[...], g_mem_ref[...]).astype(BF16)
    kv = _dot(h, w_ref[...])
    kw = MEM_HEADS * MEM_HEAD_DIM
    for hd in range(MEM_HEADS):
        sl = slice(hd * MEM_HEAD_DIM, (hd + 1) * MEM_HEAD_DIM)
        k_ref[:, sl] = _rms(kv[:, sl], g_mk_ref[...]).astype(BF16)
    v_ref[...] = kv[:, kw:].astype(BF16)


def _mixer_kernel(x_ref, pos_ref, invf_ref, g_mix_ref,
                  w_uv_ref, w_mla_ref, w_qm_ref, w_gate_ref,
                  g_ln_ref, b_ln_ref, ws_ref, bs_ref, w_ogm_ref,
                  g_cq_ref, wq_ref, g_ckv_ref, wkv_ref,
                  g_qn_ref, g_qp_ref, g_kn_ref, g_kp_ref,
                  kmem_ref, vmem_ref, g_mq_ref, w_omem_ref,
                  q_ref, k_ref, v_ref, mpart_ref, g1_ref,
                  mix_scr, att_scr):
    tm = x_ref.shape[0]
    gm_w = g_ln_ref.shape[-1]
    d_model = x_ref.shape[-1]

    h = _rms(x_ref[...], g_mix_ref[...]).astype(BF16)

    z_mla = _dot(h, w_mla_ref[...])
    z_uv = _dot(h, w_uv_ref[...])
    ang = invf_ref[...] * pos_ref[...].astype(F32)
    cos_t = jnp.cos(ang)
    sin_t = jnp.sin(ang)
    rope_tab = jnp.concatenate([cos_t, cos_t, -sin_t, sin_t], axis=0).T

    cq = _rms(z_mla[:, :Q_LORA], g_cq_ref[...]).astype(BF16)
    ckv = _rms(z_mla[:, Q_LORA:Q_LORA + KV_LORA], g_ckv_ref[...]).astype(BF16)
    kp = _rms(z_mla[:, Q_LORA + KV_LORA:], g_kp_ref[...]) * rope_tab
    kp = (kp + pltpu.roll(kp, MLA_ROPE, axis=1)).astype(BF16)

    qk_scale = math.log2(math.e) / math.sqrt(MLA_NOPE + MLA_ROPE)
    g_qn = g_qn_ref[...] * qk_scale
    q_tab = rope_tab * (g_qp_ref[...] * qk_scale)
    q_all = _dot(cq, wq_ref[...])
    kv_all = _dot(ckv, wkv_ref[...])
    g1_ref[...] = jax.nn.sigmoid(_dot(h, w_gate_ref[:, d_model:2 * d_model])).astype(BF16)
    nope_w = MLA_HEADS * MLA_NOPE
    for hd in range(MLA_HEADS):
        ns = slice(hd * MLA_NOPE, (hd + 1) * MLA_NOPE)
        ps = slice(nope_w + hd * ROPE_SLOT, nope_w + (hd + 1) * ROPE_SLOT)
        q_ref[hd, :, :MLA_NOPE] = _rms(q_all[:, ns], g_qn).astype(BF16)
        q_ref[hd, :, MLA_NOPE:] = _rms(q_all[:, ps], q_tab).astype(BF16)
        k_ref[hd, :, :MLA_NOPE] = _rms(kv_all[:, ns], g_kn_ref[...]).astype(BF16)
        k_ref[hd, :, MLA_NOPE:] = kp
    v_ref[...] = kv_all[:, nope_w:].astype(BF16)

    z_qm = _dot(h, w_qm_ref[...])
    z_g0 = _dot(h, w_gate_ref[:, :d_model])
    u = _gelu_tanh(z_uv[:, :gm_w])
    gv = _gelu_tanh(z_uv[:, gm_w:])
    mu = jnp.mean(gv, axis=-1, keepdims=True)
    gc = gv - mu
    var = jnp.mean(gc * gc, axis=-1, keepdims=True)
    v_ln = (gc * lax.rsqrt(var + EPS) * g_ln_ref[...] + b_ln_ref[...]).astype(BF16)
    row = lax.broadcasted_iota(jnp.int32, (GM_CHUNK, GM_CHUNK), 0)
    col = lax.broadcasted_iota(jnp.int32, (GM_CHUNK, GM_CHUNK), 1)
    gw = gm_w // GM_GROUPS
    for g in range(GM_GROUPS):
        w_causal = jnp.where(row >= col, ws_ref[g], 0.0).astype(BF16)
        bias = bs_ref[:, g:g + 1]
        for c in range(tm // GM_CHUNK):
            rs = slice(c * GM_CHUNK, (c + 1) * GM_CHUNK)
            cs = slice(g * gw, (g + 1) * gw)
            mix_scr[rs, cs] = _dot(w_causal, v_ln[rs, cs]) + bias
    y_gm = _dot((u * mix_scr[...]).astype(BF16), w_ogm_ref[...])
    mpart = jax.nn.sigmoid(z_g0) * y_gm

    g2 = jax.nn.sigmoid(_dot(h, w_gate_ref[:, 2 * d_model:]))
    g_mq = g_mq_ref[...] * (1.0 / math.sqrt(MEM_HEAD_DIM))
    for hd in range(MEM_HEADS):
        sl = slice(hd * MEM_HEAD_DIM, (hd + 1) * MEM_HEAD_DIM)
        qh = _rms(z_qm[:, sl], g_mq).astype(BF16)
        s = _dot_nt(qh, kmem_ref[:, sl])
        p = jnp.exp(s - jnp.max(s, axis=-1, keepdims=True))
        l = jnp.sum(p, axis=-1, keepdims=True)
        att_scr[:, sl] = _dot(p.astype(BF16), vmem_ref[:, sl]) / l
    y_mem = _dot(att_scr[...].astype(BF16), w_omem_ref[...])

    mpart_ref[...] = (mpart + g2 * y_mem).astype(BF16)


def _attn_kernel(q_ref, k_ref, v_ref, o_ref, vext_scr):
    seq, vw = v_ref.shape
    blk = ATTN_BLOCK
    vext_scr[:, :vw] = v_ref[...]
    vext_scr[:, vw:] = jnp.ones((seq, vext_scr.shape[1] - vw), BF16)
    half = blk // 2
    r = lax.broadcasted_iota(jnp.int32, (half, half), 0)
    c = lax.broadcasted_iota(jnp.int32, (half, half), 1)
    causal = r >= c
    rowmax = lambda s: jnp.max(s, axis=-1, keepdims=True)
    for i in reversed(range(seq // blk)):
        lo, mid, hi = i * blk, i * blk + half, (i + 1) * blk
        s_aa = jnp.where(causal, _dot_nt(q_ref[lo:mid, :], k_ref[lo:mid, :]), -1e30)
        s_ba = _dot_nt(q_ref[mid:hi, :], k_ref[lo:mid, :])
        s_bb = jnp.where(causal, _dot_nt(q_ref[mid:hi, :], k_ref[mid:hi, :]), -1e30)
        m_a = rowmax(s_aa)
        m_b = jnp.maximum(rowmax(s_ba), rowmax(s_bb))
        if i > 0:
            s_f = _dot_nt(q_ref[lo:hi, :], k_ref[:lo, :])
            m_f = rowmax(s_f)
            m_a = jnp.maximum(m_a, m_f[:half])
            m_b = jnp.maximum(m_b, m_f[half:])
        o_a = _dot(jnp.exp2(s_aa - m_a).astype(BF16), vext_scr[lo:mid, :])
        p_b = jnp.concatenate([jnp.exp2(s_ba - m_b), jnp.exp2(s_bb - m_b)], axis=1)
        o = jnp.concatenate([o_a, _dot(p_b.astype(BF16), vext_scr[lo:hi, :])], axis=0)
        if i > 0:
            m = jnp.concatenate([m_a, m_b], axis=0)
            o = o + _dot(jnp.exp2(s_f - m).astype(BF16), vext_scr[:lo, :])
        o_ref[lo:hi, :] = (o[:, :vw] / o[:, vw:]).astype(o_ref.dtype)


def _ffn_kernel(x_ref, a_ref, mpart_ref, g1_ref, w_omla_ref, w_out_ref, g_ffn_ref,
                w1_ref, w2_ref, o_ref):
    y_mla = _dot(a_ref[...], w_omla_ref[...])
    merged = mpart_ref[...].astype(F32) + g1_ref[...].astype(F32) * y_mla
    x1 = x_ref[...] + _dot(merged.astype(BF16), w_out_ref[...])
    h2 = _rms(x1, g_ffn_ref[...]).astype(BF16)
    acc = x1
    for c in range(w1_ref.shape[1] // FF_CHUNK):
        cs = slice(c * FF_CHUNK, (c + 1) * FF_CHUNK)
        t = jnp.maximum(_dot(h2, w1_ref[:, cs]), 0.0)
        acc = acc + _dot((t * t).astype(BF16), w2_ref[cs, :])
    o_ref[...] = acc


def _const_spec(shape):
    nd = len(shape)
    return pl.BlockSpec(shape, lambda *_: (0,) * nd, pipeline_mode=pl.Buffered(1))


def _params(sem):
    return pltpu.CompilerParams(dimension_semantics=sem,
                                vmem_limit_bytes=V7X_VMEM_LIMIT_BYTES)


def kernel(x, mem, positions, g_mix, w_in, g_cq, w_uq, g_ckv, w_ukv, g_q_nope, g_q_pe,
           g_k_nope, g_k_pe, g_gm_ln, b_gm_ln, w_spatial, b_spatial, g_mem, w_mem_kv,
           g_mq, g_mk, w_o_gm, w_o_mla, w_o_mem, w_out, g_ffn, w_ff1, w_ff2):
    B, S, D = x.shape
    M = mem.shape[1]
    T = B * S
    depth = g_mix.shape[0]
    gm_w = g_gm_ln.shape[-1]
    mem_w = MEM_HEADS * MEM_HEAD_DIM
    tm = TOKEN_TILE
    tiles_per_seq = S // tm
    assert S % tm == 0 and S % ATTN_BLOCK == 0 and (B * M) % MEM_TILE == 0

    inv_freq = (ROPE_BASE ** (-jnp.arange(0, MLA_ROPE, 2, dtype=F32) / MLA_ROPE)).reshape(-1, 1)
    pos = positions.reshape(T // tm, 1, tm)
    row = lambda a: a.reshape(1, -1)
    with_swap = lambda a: jnp.concatenate([a, _swap_halves(a)], axis=-1)

    xt = x.reshape(T, D)
    for l in range(depth):
        wi = w_in[l]
        c0 = 2 * gm_w
        c1 = c0 + Q_LORA + KV_LORA + MLA_ROPE
        c2 = c1 + mem_w
        w_uv = wi[:, :c0].astype(BF16)
        w_mla = jnp.concatenate([wi[:, c0:c1], _swap_halves(wi[:, c1 - MLA_ROPE:c1])],
                                axis=1).astype(BF16)
        w_qm = wi[:, c1:c2].astype(BF16)
        w_gate = wi[:, c2:].astype(BF16)
        wq3 = w_uq[l].reshape(Q_LORA, MLA_HEADS, MLA_NOPE + MLA_ROPE)
        wq = jnp.concatenate([wq3[:, :, :MLA_NOPE].reshape(Q_LORA, -1),
                              with_swap(wq3[:, :, MLA_NOPE:]).reshape(Q_LORA, -1)],
                             axis=1).astype(BF16)
        wkv3 = w_ukv[l].reshape(KV_LORA, MLA_HEADS, MLA_NOPE + MLA_V)
        wkv = jnp.concatenate([wkv3[:, :, :MLA_NOPE].reshape(KV_LORA, -1),
                               wkv3[:, :, MLA_NOPE:].reshape(KV_LORA, -1)], axis=1).astype(BF16)

        k_mem, v_mem = pl.pallas_call(
            _memkv_kernel,
            grid=(B * M // MEM_TILE,),
            in_specs=[pl.BlockSpec((MEM_TILE, D), lambda i: (i, 0)),
                      _const_spec((1, D)), _const_spec((D, 2 * mem_w)),
                      _const_spec((1, MEM_HEAD_DIM))],
            out_specs=[pl.BlockSpec((MEM_TILE, mem_w), lambda i: (i, 0))] * 2,
            out_shape=[jax.ShapeDtypeStruct((B * M, mem_w), BF16)] * 2,
            compiler_params=_params(("parallel",)),
            name="memkv",
        )(mem.reshape(B * M, D), row(g_mem[l]), w_mem_kv[l].astype(BF16), row(g_mk[l]))
        k_mem = k_mem.reshape(B, M, mem_w)
        v_mem = v_mem.reshape(B, M, mem_w)

        tile_map = lambda i: (i, 0)
        head_map = lambda i: (i // tiles_per_seq, 0, i % tiles_per_seq, 0)
        mem_map = lambda i: (i // tiles_per_seq, 0, 0)
        consts = [
            inv_freq, row(g_mix[l]), w_uv, w_mla, w_qm, w_gate,
            row(g_gm_ln[l]), row(b_gm_ln[l]), w_spatial[l], b_spatial[l].T,
            w_o_gm[l].astype(BF16),
            row(g_cq[l]), wq, row(g_ckv[l]), wkv,
            row(g_q_nope[l]), row(with_swap(g_q_pe[l])),
            row(g_k_nope[l]), row(with_swap(g_k_pe[l])),
        ]
        tail_consts = [row(g_mq[l]), w_o_mem[l].astype(BF16)]
        q, k, v, mpart, g1 = pl.pallas_call(
            _mixer_kernel,
            grid=(T // tm,),
            in_specs=([pl.BlockSpec((tm, D), tile_map),
                       pl.BlockSpec((None, 1, tm), lambda i: (i, 0, 0))]
                      + [_const_spec(c.shape) for c in consts]
                      + [pl.BlockSpec((None, M, mem_w), mem_map)] * 2
                      + [_const_spec(c.shape) for c in tail_consts]),
            out_specs=[pl.BlockSpec((None, MLA_HEADS, tm, QK_WIDTH), head_map),
                       pl.BlockSpec((None, MLA_HEADS, tm, QK_WIDTH), head_map),
                       pl.BlockSpec((tm, MLA_HEADS * MLA_V), tile_map),
                       pl.BlockSpec((tm, D), tile_map),
                       pl.BlockSpec((tm, D), tile_map)],
            out_shape=[jax.ShapeDtypeStruct((B, MLA_HEADS, S, QK_WIDTH), BF16),
                       jax.ShapeDtypeStruct((B, MLA_HEADS, S, QK_WIDTH), BF16),
                       jax.ShapeDtypeStruct((T, MLA_HEADS * MLA_V), BF16),
                       jax.ShapeDtypeStruct((T, D), BF16),
                       jax.ShapeDtypeStruct((T, D), BF16)],
            scratch_shapes=[pltpu.VMEM((tm, gm_w), F32), pltpu.VMEM((tm, mem_w), F32)],
            compiler_params=_params(("parallel",)),
            name="mixer",
        )(xt, pos, *consts, k_mem, v_mem, *tail_consts)

        attn = pl.pallas_call(
            _attn_kernel,
            grid=(B, MLA_HEADS),
            in_specs=[pl.BlockSpec((None, None, S, QK_WIDTH), lambda b, h: (b, h, 0, 0)),
                      pl.BlockSpec((None, None, S, QK_WIDTH), lambda b, h: (b, h, 0, 0)),
                      pl.BlockSpec((None, S, MLA_V), lambda b, h: (b, 0, h))],
            out_specs=pl.BlockSpec((None, S, MLA_V), lambda b, h: (b, 0, h)),
            out_shape=jax.ShapeDtypeStruct((B, S, MLA_HEADS * MLA_V), BF16),
            scratch_shapes=[pltpu.VMEM((S, 2 * MLA_V), BF16)],
            compiler_params=_params(("parallel", "parallel")),
            name="attn",
        )(q, k, v.reshape(B, S, MLA_HEADS * MLA_V))

        ffn_consts = [w_o_mla[l].astype(BF16), w_out[l].astype(BF16), row(g_ffn[l]),
                      w_ff1[l].astype(BF16), w_ff2[l].astype(BF16)]
        xt = pl.pallas_call(
            _ffn_kernel,
            grid=(T // tm,),
            in_specs=([pl.BlockSpec((tm, D), tile_map)] * 4
                      + [_const_spec(c.shape) for c in ffn_consts]),
            out_specs=pl.BlockSpec((tm, D), tile_map),
            out_shape=jax.ShapeDtypeStruct((T, D), F32),
            compiler_params=_params(("parallel",)),
            name="ffn",
        )(xt, attn.reshape(T, MLA_HEADS * MLA_V), mpart, g1, *ffn_consts)
    return xt.reshape(B, S, D)
```

```python
import math

import jax
import jax.numpy as jnp
from jax import lax
from jax.experimental import pallas as pl
from jax.experimental.pallas import tpu as pltpu

F32 = jnp.float32
BF16 = jnp.bfloat16

EPS = 1e-6
ROPE_BASE = 10000.0

MEM_HEADS = 4
MEM_HEAD_DIM = 128
GM_CHUNK = 128
GM_GROUPS = 4
MLA_HEADS = 8
MLA_NOPE = 128
MLA_ROPE = 64
MLA_V = 128
Q_LORA = 384
KV_LORA = 256
N_BRANCH = 3

LANE = 128
V7X_VMEM_LIMIT_BYTES = 56 * 1024 * 1024

ROPE_SLOT = 2 * MLA_ROPE
assert ROPE_SLOT == LANE
QK_WIDTH = MLA_NOPE + ROPE_SLOT

TOKEN_TILE = 512
MEM_TILE = 1024
ATTN_BLOCK = 512
ATTN_LOOKAHEAD = 1
FF_CHUNK = 1024


def _rms(x, g):
    ms = jnp.sum(x * x, axis=-1, keepdims=True) * (1.0 / x.shape[-1])
    return x * lax.rsqrt(ms + EPS) * g


def _gelu_tanh(x):
    c = math.sqrt(2.0 / math.pi)
    cdf = 0.5 * (1.0 + jnp.tanh(c * (x + 0.044715 * (x * x * x))))
    return x * cdf


def _dot(a, b):
    return jnp.dot(a, b, preferred_element_type=F32)


def _dot_nt(a, b):
    return lax.dot_general(a, b, (((1,), (1,)), ((), ())), preferred_element_type=F32)


def _swap_halves(a):
    half = a.shape[-1] // 2
    return jnp.concatenate([a[..., half:], a[..., :half]], axis=-1)


def _memkv_kernel(mem_ref, g_mem_ref, w_ref, g_mk_ref, k_ref, v_ref):
    h = _rms(mem_ref[...], g_mem_ref[...]).astype(BF16)
    kv = _dot(h, w_ref[...])
    kw = MEM_HEADS * MEM_HEAD_DIM
    for hd in range(MEM_HEADS):
        sl = slice(hd * MEM_HEAD_DIM, (hd + 1) * MEM_HEAD_DIM)
        k_ref[:, sl] = _rms(kv[:, sl], g_mk_ref[...]).astype(BF16)
    v_ref[...] = kv[:, kw:].astype(BF16)


def _mixer_kernel(x_ref, pos_ref, invf_ref, g_mix_ref,
                  w_uv_ref, w_mla_ref, w_qm_ref, w_gate_ref,
                  g_ln_ref, b_ln_ref, ws_ref, bs_ref, w_ogm_ref,
                  g_cq_ref, wq_ref, g_ckv_ref, wkv_ref,
                  g_qn_ref, g_qp_ref, g_kn_ref, g_kp_ref,
                  kmem_ref, vmem_ref, g_mq_ref, w_omem_ref,
                  q_ref, k_ref, v_ref, mpart_ref, g1_ref,
                  mix_scr, att_scr):
    tm = x_ref.shape[0]
    gm_w = g_ln_ref.shape[-1]
    d_model = x_ref.shape[-1]

    h = _rms(x_ref[...], g_mix_ref[...]).astype(BF16)

    z_mla = _dot(h, w_mla_ref[...])
    z_uv = _dot(h, w_uv_ref[...])
    ang = invf_ref[...] * pos_ref[...].astype(F32)
    cos_t = jnp.cos(ang)
    sin_t = jnp.sin(ang)
    rope_tab = jnp.concatenate([cos_t, cos_t, -sin_t, sin_t], axis=0).T

    cq = _rms(z_mla[:, :Q_LORA], g_cq_ref[...]).astype(BF16)
    ckv = _rms(z_mla[:, Q_LORA:Q_LORA + KV_LORA], g_ckv_ref[...]).astype(BF16)
    kp = _rms(z_mla[:, Q_LORA + KV_LORA:], g_kp_ref[...]) * rope_tab
    kp = (kp + pltpu.roll(kp, MLA_ROPE, axis=1)).astype(BF16)

    qk_scale = math.log2(math.e) / math.sqrt(MLA_NOPE + MLA_ROPE)
    g_qn = g_qn_ref[...] * qk_scale
    q_tab = rope_tab * (g_qp_ref[...] * qk_scale)
    q_all = _dot(cq, wq_ref[...])
    kv_all = _dot(ckv, wkv_ref[...])
    g1_ref[...] = jax.nn.sigmoid(_dot(h, w_gate_ref[:, d_model:2 * d_model])).astype(BF16)
    nope_w = MLA_HEADS * MLA_NOPE
    for hd in range(MLA_HEADS):
        ns = slice(hd * MLA_NOPE, (hd + 1) * MLA_NOPE)
        ps = slice(nope_w + hd * ROPE_SLOT, nope_w + (hd + 1) * ROPE_SLOT)
        q_ref[hd, :, :MLA_NOPE] = _rms(q_all[:, ns], g_qn).astype(BF16)
        q_ref[hd, :, MLA_NOPE:] = _rms(q_all[:, ps], q_tab).astype(BF16)
        k_ref[hd, :, :MLA_NOPE] = _rms(kv_all[:, ns], g_kn_ref[...]).astype(BF16)
        k_ref[hd, :, MLA_NOPE:] = kp
        v_ref[hd] = kv_all[:, nope_w + hd * MLA_V:nope_w + (hd + 1) * MLA_V].astype(BF16)

    z_qm = _dot(h, w_qm_ref[...])
    z_g0 = _dot(h, w_gate_ref[:, :d_model])
    u = _gelu_tanh(z_uv[:, :gm_w])
    gv = _gelu_tanh(z_uv[:, gm_w:])
    mu = jnp.mean(gv, axis=-1, keepdims=True)
    gc = gv - mu
    var = jnp.mean(gc * gc, axis=-1, keepdims=True)
    v_ln = (gc * lax.rsqrt(var + EPS) * g_ln_ref[...] + b_ln_ref[...]).astype(BF16)
    row = lax.broadcasted_iota(jnp.int32, (GM_CHUNK, GM_CHUNK), 0)
    col = lax.broadcasted_iota(jnp.int32, (GM_CHUNK, GM_CHUNK), 1)
    gw = gm_w // GM_GROUPS
    for g in range(GM_GROUPS):
        w_causal = jnp.where(row >= col, ws_ref[g], 0.0).astype(BF16)
        bias = bs_ref[:, g:g + 1]
        for c in range(tm // GM_CHUNK):
            rs = slice(c * GM_CHUNK, (c + 1) * GM_CHUNK)
            cs = slice(g * gw, (g + 1) * gw)
            mix_scr[rs, cs] = _dot(w_causal, v_ln[rs, cs]) + bias
    y_gm = _dot((u * mix_scr[...]).astype(BF16), w_ogm_ref[...])
    mpart = jax.nn.sigmoid(z_g0) * y_gm

    g2 = jax.nn.sigmoid(_dot(h, w_gate_ref[:, 2 * d_model:]))
    g_mq = g_mq_ref[...] * (1.0 / math.sqrt(MEM_HEAD_DIM))
    for hd in range(MEM_HEADS):
        sl = slice(hd * MEM_HEAD_DIM, (hd + 1) * MEM_HEAD_DIM)
        qh = _rms(z_qm[:, sl], g_mq).astype(BF16)
        s = _dot_nt(qh, kmem_ref[:, sl])
        p = jnp.exp(s - jnp.max(s, axis=-1, keepdims=True))
        l = jnp.sum(p, axis=-1, keepdims=True)
        att_scr[:, sl] = _dot(p.astype(BF16), vmem_ref[:, sl]) / l
    y_mem = _dot(att_scr[...].astype(BF16), w_omem_ref[...])

    mpart_ref[...] = (mpart + g2 * y_mem).astype(BF16)


def _attn_kernel(q_ref, k_ref, v_ref, o_ref, vext_scr):
    seq, vw = v_ref.shape
    blk = ATTN_BLOCK
    vext_scr[:, :vw] = v_ref[...]
    vext_scr[:, vw:] = jnp.ones((seq, vext_scr.shape[1] - vw), BF16)
    half = blk // 2
    r = lax.broadcasted_iota(jnp.int32, (half, half), 0)
    c = lax.broadcasted_iota(jnp.int32, (half, half), 1)
    causal = r >= c
    rowmax = lambda s: jnp.max(s, axis=-1, keepdims=True)

    def scores(i):
        lo, mid, hi = i * blk, i * blk + half, (i + 1) * blk
        s_aa = jnp.where(causal, _dot_nt(q_ref[lo:mid, :], k_ref[lo:mid, :]), -1e30)
        s_ba = _dot_nt(q_ref[mid:hi, :], k_ref[lo:mid, :])
        s_bb = jnp.where(causal, _dot_nt(q_ref[mid:hi, :], k_ref[mid:hi, :]), -1e30)
        s_f = _dot_nt(q_ref[lo:hi, :], k_ref[:lo, :]) if i > 0 else None
        return s_aa, s_ba, s_bb, s_f

    def finish(i, s_aa, s_ba, s_bb, s_f):
        lo, mid, hi = i * blk, i * blk + half, (i + 1) * blk
        m_a = rowmax(s_aa)
        m_b = jnp.maximum(rowmax(s_ba), rowmax(s_bb))
        if s_f is not None:
            m_f = rowmax(s_f)
            m_a = jnp.maximum(m_a, m_f[:half])
            m_b = jnp.maximum(m_b, m_f[half:])
        o_a = _dot(jnp.exp2(s_aa - m_a).astype(BF16), vext_scr[lo:mid, :])
        p_b = jnp.concatenate([jnp.exp2(s_ba - m_b), jnp.exp2(s_bb - m_b)], axis=1)
        o = jnp.concatenate([o_a, _dot(p_b.astype(BF16), vext_scr[lo:hi, :])], axis=0)
        if s_f is not None:
            m = jnp.concatenate([m_a, m_b], axis=0)
            o = o + _dot(jnp.exp2(s_f - m).astype(BF16), vext_scr[:lo, :])
        o_ref[lo:hi, :] = (o[:, :vw] / o[:, vw:]).astype(o_ref.dtype)

    order = list(reversed(range(seq // blk)))
    pending = [scores(i) for i in order[:ATTN_LOOKAHEAD]]
    for n, i in enumerate(order):
        if n + ATTN_LOOKAHEAD < len(order):
            pending.append(scores(order[n + ATTN_LOOKAHEAD]))
        finish(i, *pending.pop(0))


def _ffn_kernel(x_ref, a_ref, mpart_ref, g1_ref, w_omla_ref, w_out_ref, g_ffn_ref,
                w1_ref, w2_ref, o_ref):
    attn = jnp.concatenate([a_ref[hd] for hd in range(a_ref.shape[0])], axis=1)
    y_mla = _dot(attn, w_omla_ref[...])
    merged = mpart_ref[...].astype(F32) + g1_ref[...].astype(F32) * y_mla
    x1 = x_ref[...] + _dot(merged.astype(BF16), w_out_ref[...])
    h2 = _rms(x1, g_ffn_ref[...]).astype(BF16)
    acc = x1
    for c in range(w1_ref.shape[1] // FF_CHUNK):
        cs = slice(c * FF_CHUNK, (c + 1) * FF_CHUNK)
        t = jnp.maximum(_dot(h2, w1_ref[:, cs]), 0.0)
        acc = acc + _dot((t * t).astype(BF16), w2_ref[cs, :])
    o_ref[...] = acc


def _const_spec(shape):
    nd = len(shape)
    return pl.BlockSpec(shape, lambda *_: (0,) * nd, pipeline_mode=pl.Buffered(1))


def _params(sem):
    return pltpu.CompilerParams(dimension_semantics=sem,
                                vmem_limit_bytes=V7X_VMEM_LIMIT_BYTES)


def kernel(x, mem, positions, g_mix, w_in, g_cq, w_uq, g_ckv, w_ukv, g_q_nope, g_q_pe,
           g_k_nope, g_k_pe, g_gm_ln, b_gm_ln, w_spatial, b_spatial, g_mem, w_mem_kv,
           g_mq, g_mk, w_o_gm, w_o_mla, w_o_mem, w_out, g_ffn, w_ff1, w_ff2):
    B, S, D = x.shape
    M = mem.shape[1]
    T = B * S
    depth = g_mix.shape[0]
    gm_w = g_gm_ln.shape[-1]
    mem_w = MEM_HEADS * MEM_HEAD_DIM
    tm = TOKEN_TILE
    tiles_per_seq = S // tm
    assert S % tm == 0 and S % ATTN_BLOCK == 0 and (B * M) % MEM_TILE == 0

    inv_freq = (ROPE_BASE ** (-jnp.arange(0, MLA_ROPE, 2, dtype=F32) / MLA_ROPE)).reshape(-1, 1)
    pos = positions.reshape(T // tm, 1, tm)
    row = lambda a: a.reshape(1, -1)
    with_swap = lambda a: jnp.concatenate([a, _swap_halves(a)], axis=-1)

    xt = x.reshape(T, D)
    for l in range(depth):
        wi = w_in[l]
        c0 = 2 * gm_w
        c1 = c0 + Q_LORA + KV_LORA + MLA_ROPE
        c2 = c1 + mem_w
        w_uv = wi[:, :c0].astype(BF16)
        w_mla = jnp.concatenate([wi[:, c0:c1], _swap_halves(wi[:, c1 - MLA_ROPE:c1])],
                                axis=1).astype(BF16)
        w_qm = wi[:, c1:c2].astype(BF16)
        w_gate = wi[:, c2:].astype(BF16)
        wq3 = w_uq[l].reshape(Q_LORA, MLA_HEADS, MLA_NOPE + MLA_ROPE)
        wq = jnp.concatenate([wq3[:, :, :MLA_NOPE].reshape(Q_LORA, -1),
                              with_swap(wq3[:, :, MLA_NOPE:]).reshape(Q_LORA, -1)],
                             axis=1).astype(BF16)
        wkv3 = w_ukv[l].reshape(KV_LORA, MLA_HEADS, MLA_NOPE + MLA_V)
        wkv = jnp.concatenate([wkv3[:, :, :MLA_NOPE].reshape(KV_LORA, -1),
                               wkv3[:, :, MLA_NOPE:].reshape(KV_LORA, -1)], axis=1).astype(BF16)

        k_mem, v_mem = pl.pallas_call(
            _memkv_kernel,
            grid=(B * M // MEM_TILE,),
            in_specs=[pl.BlockSpec((MEM_TILE, D), lambda i: (i, 0)),
                      _const_spec((1, D)), _const_spec((D, 2 * mem_w)),
                      _const_spec((1, MEM_HEAD_DIM))],
            out_specs=[pl.BlockSpec((MEM_TILE, mem_w), lambda i: (i, 0))] * 2,
            out_shape=[jax.ShapeDtypeStruct((B * M, mem_w), BF16)] * 2,
            compiler_params=_params(("parallel",)),
            name="memkv",
        )(mem.reshape(B * M, D), row(g_mem[l]), w_mem_kv[l].astype(BF16), row(g_mk[l]))
        k_mem = k_mem.reshape(B, M, mem_w)
        v_mem = v_mem.reshape(B, M, mem_w)

        tile_map = lambda i: (i, 0)
        head_map = lambda i: (i // tiles_per_seq, 0, i % tiles_per_seq, 0)
        mem_map = lambda i: (i // tiles_per_seq, 0, 0)
        consts = [
            inv_freq, row(g_mix[l]), w_uv, w_mla, w_qm, w_gate,
            row(g_gm_ln[l]), row(b_gm_ln[l]), w_spatial[l], b_spatial[l].T,
            w_o_gm[l].astype(BF16),
            row(g_cq[l]), wq, row(g_ckv[l]), wkv,
            row(g_q_nope[l]), row(with_swap(g_q_pe[l])),
            row(g_k_nope[l]), row(with_swap(g_k_pe[l])),
        ]
        tail_consts = [row(g_mq[l]), w_o_mem[l].astype(BF16)]
        q, k, v, mpart, g1 = pl.pallas_call(
            _mixer_kernel,
            grid=(T // tm,),
            in_specs=([pl.BlockSpec((tm, D), tile_map),
                       pl.BlockSpec((None, 1, tm), lambda i: (i, 0, 0))]
                      + [_const_spec(c.shape) for c in consts]
                      + [pl.BlockSpec((None, M, mem_w), mem_map)] * 2
                      + [_const_spec(c.shape) for c in tail_consts]),
            out_specs=[pl.BlockSpec((None, MLA_HEADS, tm, QK_WIDTH), head_map),
                       pl.BlockSpec((None, MLA_HEADS, tm, QK_WIDTH), head_map),
                       pl.BlockSpec((None, MLA_HEADS, tm, MLA_V), head_map),
                       pl.BlockSpec((tm, D), tile_map),
                       pl.BlockSpec((tm, D), tile_map)],
            out_shape=[jax.ShapeDtypeStruct((B, MLA_HEADS, S, QK_WIDTH), BF16),
                       jax.ShapeDtypeStruct((B, MLA_HEADS, S, QK_WIDTH), BF16),
                       jax.ShapeDtypeStruct((B, MLA_HEADS, S, MLA_V), BF16),
                       jax.ShapeDtypeStruct((T, D), BF16),
                       jax.ShapeDtypeStruct((T, D), BF16)],
            scratch_shapes=[pltpu.VMEM((tm, gm_w), F32), pltpu.VMEM((tm, mem_w), F32)],
            compiler_params=_params(("parallel",)),
            name="mixer",
        )(xt, pos, *consts, k_mem, v_mem, *tail_consts)

        bh_map = lambda b, h: (b, h, 0, 0)
        attn = pl.pallas_call(
            _attn_kernel,
            grid=(B, MLA_HEADS),
            in_specs=[pl.BlockSpec((None, None, S, QK_WIDTH), bh_map),
                      pl.BlockSpec((None, None, S, QK_WIDTH), bh_map),
                      pl.BlockSpec((None, None, S, MLA_V), bh_map)],
            out_specs=pl.BlockSpec((None, None, S, MLA_V), bh_map),
            out_shape=jax.ShapeDtypeStruct((B, MLA_HEADS, S, MLA_V), BF16),
            scratch_shapes=[pltpu.VMEM((S, 2 * MLA_V), BF16)],
            compiler_params=_params(("parallel", "parallel")),
            name="attn",
        )(q, k, v)

        ffn_consts = [w_o_mla[l].astype(BF16), w_out[l].astype(BF16), row(g_ffn[l]),
                      w_ff1[l].astype(BF16), w_ff2[l].astype(BF16)]
        xt = pl.pallas_call(
            _ffn_kernel,
            grid=(T // tm,),
            in_specs=([pl.BlockSpec((tm, D), tile_map),
                       pl.BlockSpec((None, MLA_HEADS, tm, MLA_V), head_map)]
                      + [pl.BlockSpec((tm, D), tile_map)] * 2
                      + [_const_spec(c.shape) for c in ffn_consts]),
            out_specs=pl.BlockSpec((tm, D), tile_map),
            out_shape=jax.ShapeDtypeStruct((T, D), F32),
            compiler_params=_params(("parallel",)),
            name="ffn",
        )(xt, attn, mpart, g1, *ffn_consts)
    return xt.reshape(B, S, D)
```

```python
import math

import jax
import jax.numpy as jnp
from jax import lax
from jax.experimental import pallas as pl
from jax.experimental.pallas import tpu as pltpu

F32 = jnp.float32
BF16 = jnp.bfloat16

EPS = 1e-6
ROPE_BASE = 10000.0

MEM_HEADS = 4
MEM_HEAD_DIM = 128
GM_CHUNK = 128
GM_GROUPS = 4
MLA_HEADS = 8
MLA_NOPE = 128
MLA_ROPE = 64
MLA_V = 128
Q_LORA = 384
KV_LORA = 256
N_BRANCH = 3

LANE = 128
V7X_VMEM_LIMIT_BYTES = 56 * 1024 * 1024

ROPE_SLOT = 2 * MLA_ROPE
assert ROPE_SLOT == LANE
QK_WIDTH = MLA_NOPE + ROPE_SLOT

TOKEN_TILE = 512
MEM_TILE = 1024
ATTN_BLOCK = 256
ATTN_LOOKAHEAD = 2
ONES_ROWS = 16
FF_CHUNK = 1024


def _rms(x, g):
    ms = jnp.sum(x * x, axis=-1, keepdims=True) * (1.0 / x.shape[-1])
    return x * lax.rsqrt(ms + EPS) * g


def _gelu_tanh(x):
    c = math.sqrt(2.0 / math.pi)
    cdf = 0.5 * (1.0 + jnp.tanh(c * (x + 0.044715 * (x * x * x))))
    return x * cdf


def _dot(a, b):
    return jnp.dot(a, b, preferred_element_type=F32)


def _dot_nt(a, b):
    return lax.dot_general(a, b, (((1,), (1,)), ((), ())), preferred_element_type=F32)


def _swap_halves(a):
    half = a.shape[-1] // 2
    return jnp.concatenate([a[..., half:], a[..., :half]], axis=-1)


def _memkv_kernel(mem_ref, g_mem_ref, w_ref, g_mk_ref, k_ref, v_ref):
    h = _rms(mem_ref[...], g_mem_ref[...]).astype(BF16)
    kv = _dot(h, w_ref[...])
    kw = MEM_HEADS * MEM_HEAD_DIM
    for hd in range(MEM_HEADS):
        sl = slice(hd * MEM_HEAD_DIM, (hd + 1) * MEM_HEAD_DIM)
        k_ref[:, sl] = _rms(kv[:, sl], g_mk_ref[...]).astype(BF16)
    v_ref[...] = kv[:, kw:].astype(BF16)


def _mixer_kernel(x_ref, pos_ref, invf_ref, g_mix_ref,
                  w_uv_ref, w_mla_ref, w_qm_ref, w_gate_ref,
                  g_ln_ref, b_ln_ref, ws_ref, bs_ref, w_ogm_ref,
                  g_cq_ref, wq_ref, g_ckv_ref, wkv_ref,
                  g_qn_ref, g_qp_ref, g_kn_ref, g_kp_ref,
                  kmem_ref, vmem_ref, g_mq_ref, w_omem_ref,
                  q_ref, k_ref, v_ref, mpart_ref, g1_ref,
                  mix_scr, att_scr):
    tm = x_ref.shape[0]
    gm_w = g_ln_ref.shape[-1]
    d_model = x_ref.shape[-1]

    h = _rms(x_ref[...], g_mix_ref[...]).astype(BF16)

    z_mla = _dot(h, w_mla_ref[...])
    z_uv = _dot(h, w_uv_ref[...])
    ang = invf_ref[...] * pos_ref[...].astype(F32)
    cos_t = jnp.cos(ang)
    sin_t = jnp.sin(ang)
    rope_tab = jnp.concatenate([cos_t, cos_t, -sin_t, sin_t], axis=0).T

    cq = _rms(z_mla[:, :Q_LORA], g_cq_ref[...]).astype(BF16)
    ckv = _rms(z_mla[:, Q_LORA:Q_LORA + KV_LORA], g_ckv_ref[...]).astype(BF16)
    kp = _rms(z_mla[:, Q_LORA + KV_LORA:], g_kp_ref[...]) * rope_tab
    kp = (kp + pltpu.roll(kp, MLA_ROPE, axis=1)).astype(BF16)

    qk_scale = math.log2(math.e) / math.sqrt(MLA_NOPE + MLA_ROPE)
    g_qn = g_qn_ref[...] * qk_scale
    q_tab = rope_tab * (g_qp_ref[...] * qk_scale)
    q_all = _dot(cq, wq_ref[...])
    kv_all = _dot(ckv, wkv_ref[...])
    g1_ref[...] = jax.nn.sigmoid(_dot(h, w_gate_ref[:, d_model:2 * d_model])).astype(BF16)
    nope_w = MLA_HEADS * MLA_NOPE
    for hd in range(MLA_HEADS):
        ns = slice(hd * MLA_NOPE, (hd + 1) * MLA_NOPE)
        ps = slice(nope_w + hd * ROPE_SLOT, nope_w + (hd + 1) * ROPE_SLOT)
        q_ref[hd, :, :MLA_NOPE] = _rms(q_all[:, ns], g_qn).astype(BF16)
        q_ref[hd, :, MLA_NOPE:] = _rms(q_all[:, ps], q_tab).astype(BF16)
        k_ref[hd, :, :MLA_NOPE] = _rms(kv_all[:, ns], g_kn_ref[...]).astype(BF16)
        k_ref[hd, :, MLA_NOPE:] = kp
        v_ref[hd] = kv_all[:, nope_w + hd * MLA_V:nope_w + (hd + 1) * MLA_V].astype(BF16)

    z_qm = _dot(h, w_qm_ref[...])
    z_g0 = _dot(h, w_gate_ref[:, :d_model])
    u = _gelu_tanh(z_uv[:, :gm_w])
    gv = _gelu_tanh(z_uv[:, gm_w:])
    mu = jnp.mean(gv, axis=-1, keepdims=True)
    gc = gv - mu
    var = jnp.mean(gc * gc, axis=-1, keepdims=True)
    v_ln = (gc * lax.rsqrt(var + EPS) * g_ln_ref[...] + b_ln_ref[...]).astype(BF16)
    row = lax.broadcasted_iota(jnp.int32, (GM_CHUNK, GM_CHUNK), 0)
    col = lax.broadcasted_iota(jnp.int32, (GM_CHUNK, GM_CHUNK), 1)
    gw = gm_w // GM_GROUPS
    for g in range(GM_GROUPS):
        w_causal = jnp.where(row >= col, ws_ref[g], 0.0).astype(BF16)
        bias = bs_ref[:, g:g + 1]
        for c in range(tm // GM_CHUNK):
            rs = slice(c * GM_CHUNK, (c + 1) * GM_CHUNK)
            cs = slice(g * gw, (g + 1) * gw)
            mix_scr[rs, cs] = _dot(w_causal, v_ln[rs, cs]) + bias
    y_gm = _dot((u * mix_scr[...]).astype(BF16), w_ogm_ref[...])
    mpart = jax.nn.sigmoid(z_g0) * y_gm

    g2 = jax.nn.sigmoid(_dot(h, w_gate_ref[:, 2 * d_model:]))
    g_mq = g_mq_ref[...] * (1.0 / math.sqrt(MEM_HEAD_DIM))
    for hd in range(MEM_HEADS):
        sl = slice(hd * MEM_HEAD_DIM, (hd + 1) * MEM_HEAD_DIM)
        qh = _rms(z_qm[:, sl], g_mq).astype(BF16)
        s = _dot_nt(qh, kmem_ref[:, sl])
        p = jnp.exp(s - jnp.max(s, axis=-1, keepdims=True))
        l = jnp.sum(p, axis=-1, keepdims=True)
        att_scr[:, sl] = _dot(p.astype(BF16), vmem_ref[:, sl]) / l
    y_mem = _dot(att_scr[...].astype(BF16), w_omem_ref[...])

    mpart_ref[...] = (mpart + g2 * y_mem).astype(BF16)


def _attn_kernel(q_ref, k_ref, v_ref, o_ref, vt_scr):
    seq, vw = v_ref.shape
    blk = ATTN_BLOCK
    vt_scr[:vw, :] = v_ref[...].astype(F32).T.astype(BF16)
    vt_scr[vw:, :] = jnp.ones((vt_scr.shape[0] - vw, seq), BF16)
    r = lax.broadcasted_iota(jnp.int32, (blk, blk), 0)
    c = lax.broadcasted_iota(jnp.int32, (blk, blk), 1)
    causal = c >= r
    colmax = lambda s: jnp.max(s, axis=0, keepdims=True)

    def scores(i):
        lo, hi = i * blk, (i + 1) * blk
        s_d = jnp.where(causal, _dot_nt(k_ref[lo:hi, :], q_ref[lo:hi, :]), -1e30)
        s_f = _dot_nt(k_ref[:lo, :], q_ref[lo:hi, :]) if i > 0 else None
        return s_d, s_f

    def finish(i, s_d, s_f):
        lo, hi = i * blk, (i + 1) * blk
        m = colmax(s_d)
        if s_f is not None:
            m = jnp.maximum(m, colmax(s_f))
        o = _dot(vt_scr[:, lo:hi], jnp.exp2(s_d - m).astype(BF16))
        if s_f is not None:
            o = o + _dot(vt_scr[:, :lo], jnp.exp2(s_f - m).astype(BF16))
        o_ref[lo:hi, :] = (o[:vw, :] / o[vw:vw + 1, :]).T.astype(o_ref.dtype)

    order = list(reversed(range(seq // blk)))
    pending = [scores(i) for i in order[:ATTN_LOOKAHEAD]]
    for n, i in enumerate(order):
        if n + ATTN_LOOKAHEAD < len(order):
            pending.append(scores(order[n + ATTN_LOOKAHEAD]))
        finish(i, *pending.pop(0))


def _ffn_kernel(x_ref, a_ref, mpart_ref, g1_ref, w_omla_ref, w_out_ref, g_ffn_ref,
                w1_ref, w2_ref, o_ref):
    attn = jnp.concatenate([a_ref[hd] for hd in range(a_ref.shape[0])], axis=1)
    y_mla = _dot(attn, w_omla_ref[...])
    merged = mpart_ref[...].astype(F32) + g1_ref[...].astype(F32) * y_mla
    x1 = x_ref[...] + _dot(merged.astype(BF16), w_out_ref[...])
    h2 = _rms(x1, g_ffn_ref[...]).astype(BF16)
    acc = x1
    for c in range(w1_ref.shape[1] // FF_CHUNK):
        cs = slice(c * FF_CHUNK, (c + 1) * FF_CHUNK)
        t = jnp.maximum(_dot(h2, w1_ref[:, cs]), 0.0)
        acc = acc + _dot((t * t).astype(BF16), w2_ref[cs, :])
    o_ref[...] = acc


def _const_spec(shape):
    nd = len(shape)
    return pl.BlockSpec(shape, lambda *_: (0,) * nd, pipeline_mode=pl.Buffered(1))


def _params(sem):
    return pltpu.CompilerParams(dimension_semantics=sem,
                                vmem_limit_bytes=V7X_VMEM_LIMIT_BYTES)


def kernel(x, mem, positions, g_mix, w_in, g_cq, w_uq, g_ckv, w_ukv, g_q_nope, g_q_pe,
           g_k_nope, g_k_pe, g_gm_ln, b_gm_ln, w_spatial, b_spatial, g_mem, w_mem_kv,
           g_mq, g_mk, w_o_gm, w_o_mla, w_o_mem, w_out, g_ffn, w_ff1, w_ff2):
    B, S, D = x.shape
    M = mem.shape[1]
    T = B * S
    depth = g_mix.shape[0]
    gm_w = g_gm_ln.shape[-1]
    mem_w = MEM_HEADS * MEM_HEAD_DIM
    tm = TOKEN_TILE
    tiles_per_seq = S // tm
    assert S % tm == 0 and S % ATTN_BLOCK == 0 and (B * M) % MEM_TILE == 0

    inv_freq = (ROPE_BASE ** (-jnp.arange(0, MLA_ROPE, 2, dtype=F32) / MLA_ROPE)).reshape(-1, 1)
    pos = positions.reshape(T // tm, 1, tm)
    row = lambda a: a.reshape(1, -1)
    with_swap = lambda a: jnp.concatenate([a, _swap_halves(a)], axis=-1)

    xt = x.reshape(T, D)
    for l in range(depth):
        wi = w_in[l]
        c0 = 2 * gm_w
        c1 = c0 + Q_LORA + KV_LORA + MLA_ROPE
        c2 = c1 + mem_w
        w_uv = wi[:, :c0].astype(BF16)
        w_mla = jnp.concatenate([wi[:, c0:c1], _swap_halves(wi[:, c1 - MLA_ROPE:c1])],
                                axis=1).astype(BF16)
        w_qm = wi[:, c1:c2].astype(BF16)
        w_gate = wi[:, c2:].astype(BF16)
        wq3 = w_uq[l].reshape(Q_LORA, MLA_HEADS, MLA_NOPE + MLA_ROPE)
        wq = jnp.concatenate([wq3[:, :, :MLA_NOPE].reshape(Q_LORA, -1),
                              with_swap(wq3[:, :, MLA_NOPE:]).reshape(Q_LORA, -1)],
                             axis=1).astype(BF16)
        wkv3 = w_ukv[l].reshape(KV_LORA, MLA_HEADS, MLA_NOPE + MLA_V)
        wkv = jnp.concatenate([wkv3[:, :, :MLA_NOPE].reshape(KV_LORA, -1),
                               wkv3[:, :, MLA_NOPE:].reshape(KV_LORA, -1)], axis=1).astype(BF16)

        k_mem, v_mem = pl.pallas_call(
            _memkv_kernel,
            grid=(B * M // MEM_TILE,),
            in_specs=[pl.BlockSpec((MEM_TILE, D), lambda i: (i, 0)),
                      _const_spec((1, D)), _const_spec((D, 2 * mem_w)),
                      _const_spec((1, MEM_HEAD_DIM))],
            out_specs=[pl.BlockSpec((MEM_TILE, mem_w), lambda i: (i, 0))] * 2,
            out_shape=[jax.ShapeDtypeStruct((B * M, mem_w), BF16)] * 2,
            compiler_params=_params(("parallel",)),
            name="memkv",
        )(mem.reshape(B * M, D), row(g_mem[l]), w_mem_kv[l].astype(BF16), row(g_mk[l]))
        k_mem = k_mem.reshape(B, M, mem_w)
        v_mem = v_mem.reshape(B, M, mem_w)

        tile_map = lambda i: (i, 0)
        head_map = lambda i: (i // tiles_per_seq, 0, i % tiles_per_seq, 0)
        mem_map = lambda i: (i // tiles_per_seq, 0, 0)
        consts = [
            inv_freq, row(g_mix[l]), w_uv, w_mla, w_qm, w_gate,
            row(g_gm_ln[l]), row(b_gm_ln[l]), w_spatial[l], b_spatial[l].T,
            w_o_gm[l].astype(BF16),
            row(g_cq[l]), wq, row(g_ckv[l]), wkv,
            row(g_q_nope[l]), row(with_swap(g_q_pe[l])),
            row(g_k_nope[l]), row(with_swap(g_k_pe[l])),
        ]
        tail_consts = [row(g_mq[l]), w_o_mem[l].astype(BF16)]
        q, k, v, mpart, g1 = pl.pallas_call(
            _mixer_kernel,
            grid=(T // tm,),
            in_specs=([pl.BlockSpec((tm, D), tile_map),
                       pl.BlockSpec((None, 1, tm), lambda i: (i, 0, 0))]
                      + [_const_spec(c.shape) for c in consts]
                      + [pl.BlockSpec((None, M, mem_w), mem_map)] * 2
                      + [_const_spec(c.shape) for c in tail_consts]),
            out_specs=[pl.BlockSpec((None, MLA_HEADS, tm, QK_WIDTH), head_map),
                       pl.BlockSpec((None, MLA_HEADS, tm, QK_WIDTH), head_map),
                       pl.BlockSpec((None, MLA_HEADS, tm, MLA_V), head_map),
                       pl.BlockSpec((tm, D), tile_map),
                       pl.BlockSpec((tm, D), tile_map)],
            out_shape=[jax.ShapeDtypeStruct((B, MLA_HEADS, S, QK_WIDTH), BF16),
                       jax.ShapeDtypeStruct((B, MLA_HEADS, S, QK_WIDTH), BF16),
                       jax.ShapeDtypeStruct((B, MLA_HEADS, S, MLA_V), BF16),
                       jax.ShapeDtypeStruct((T, D), BF16),
                       jax.ShapeDtypeStruct((T, D), BF16)],
            scratch_shapes=[pltpu.VMEM((tm, gm_w), F32), pltpu.VMEM((tm, mem_w), F32)],
            compiler_params=_params(("parallel",)),
            name="mixer",
        )(xt, pos, *consts, k_mem, v_mem, *tail_consts)

        bh_map = lambda b, h: (b, h, 0, 0)
        attn = pl.pallas_call(
            _attn_kernel,
            grid=(B, MLA_HEADS),
            in_specs=[pl.BlockSpec((None, None, S, QK_WIDTH), bh_map),
                      pl.BlockSpec((None, None, S, QK_WIDTH), bh_map),
                      pl.BlockSpec((None, None, S, MLA_V), bh_map)],
            out_specs=pl.BlockSpec((None, None, S, MLA_V), bh_map),
            out_shape=jax.ShapeDtypeStruct((B, MLA_HEADS, S, MLA_V), BF16),
            scratch_shapes=[pltpu.VMEM((MLA_V + ONES_ROWS, S), BF16)],
            compiler_params=_params(("parallel", "parallel")),
            name="attn",
        )(q, k, v)

        ffn_consts = [w_o_mla[l].astype(BF16), w_out[l].astype(BF16), row(g_ffn[l]),
                      w_ff1[l].astype(BF16), w_ff2[l].astype(BF16)]
        xt = pl.pallas_call(
            _ffn_kernel,
            grid=(T // tm,),
            in_specs=([pl.BlockSpec((tm, D), tile_map),
                       pl.BlockSpec((None, MLA_HEADS, tm, MLA_V), head_map)]
                      + [pl.BlockSpec((tm, D), tile_map)] * 2
                      + [_const_spec(c.shape) for c in ffn_consts]),
            out_specs=pl.BlockSpec((tm, D), tile_map),
            out_shape=jax.ShapeDtypeStruct((T, D), F32),
            compiler_params=_params(("parallel",)),
            name="ffn",
        )(xt, attn, mpart, g1, *ffn_consts)
    return xt.reshape(B, S, D)
```

```python
import math

import jax
import jax.numpy as jnp
from jax import lax
from jax.experimental import pallas as pl
from jax.experimental.pallas import tpu as pltpu

F32 = jnp.float32
BF16 = jnp.bfloat16

EPS = 1e-6
ROPE_BASE = 10000.0

MEM_HEADS = 4
MEM_HEAD_DIM = 128
GM_CHUNK = 128
GM_GROUPS = 4
MLA_HEADS = 8
MLA_NOPE = 128
MLA_ROPE = 64
MLA_V = 128
Q_LORA = 384
KV_LORA = 256
N_BRANCH = 3

LANE = 128
V7X_VMEM_LIMIT_BYTES = 56 * 1024 * 1024

ROPE_SLOT = 2 * MLA_ROPE
assert ROPE_SLOT == LANE
QK_WIDTH = MLA_NOPE + ROPE_SLOT

TOKEN_TILE = 512
MEM_TILE = 1024
ATTN_BLOCK = 512
ATTN_LOOKAHEAD = 1
FF_CHUNK = 1024


def _rms(x, g):
    ms = jnp.sum(x * x, axis=-1, keepdims=True) * (1.0 / x.shape[-1])
    return x * lax.rsqrt(ms + EPS) * g


def _gelu_tanh(x):
    c = math.sqrt(2.0 / math.pi)
    cdf = 0.5 * (1.0 + jnp.tanh(c * (x + 0.044715 * (x * x * x))))
    return x * cdf


def _sigmoid(x):
    return 0.5 * jnp.tanh(0.5 * x) + 0.5


def _dot(a, b):
    return jnp.dot(a, b, preferred_element_type=F32)


def _dot_nt(a, b):
    return lax.dot_general(a, b, (((1,), (1,)), ((), ())), preferred_element_type=F32)


def _swap_halves(a):
    half = a.shape[-1] // 2
    return jnp.concatenate([a[..., half:], a[..., :half]], axis=-1)


def _memkv_kernel(mem_ref, g_mem_ref, w_ref, g_mk_ref, k_ref, v_ref):
    h = _rms(mem_ref[...], g_mem_ref[...]).astype(BF16)
    kv = _dot(h, w_ref[...])
    kw = MEM_HEADS * MEM_HEAD_DIM
    for hd in range(MEM_HEADS):
        sl = slice(hd * MEM_HEAD_DIM, (hd + 1) * MEM_HEAD_DIM)
        k_ref[:, sl] = _rms(kv[:, sl], g_mk_ref[...]).astype(BF16)
    v_ref[...] = kv[:, kw:].astype(BF16)


def _mixer_kernel(x_ref, pos_ref, invf_ref, g_mix_ref,
                  w_uv_ref, w_mla_ref, w_qm_ref, w_gate_ref,
                  g_ln_ref, b_ln_ref, ws_ref, bs_ref, w_ogm_ref,
                  g_cq_ref, wq_ref, g_ckv_ref, wkv_ref,
                  g_qn_ref, g_qp_ref, g_kn_ref, g_kp_ref,
                  kmem_ref, vmem_ref, g_mq_ref, w_omem_ref,
                  q_ref, k_ref, v_ref, mpart_ref, g1_ref,
                  mix_scr, att_scr):
    tm = x_ref.shape[0]
    gm_w = g_ln_ref.shape[-1]
    d_model = x_ref.shape[-1]

    h = _rms(x_ref[...], g_mix_ref[...]).astype(BF16)

    z_mla = _dot(h, w_mla_ref[...])
    z_uv = _dot(h, w_uv_ref[...])
    ang = invf_ref[...] * pos_ref[...].astype(F32)
    cos_t = jnp.cos(ang)
    sin_t = jnp.sin(ang)
    rope_tab = jnp.concatenate([cos_t, cos_t, -sin_t, sin_t], axis=0).T

    cq = _rms(z_mla[:, :Q_LORA], g_cq_ref[...]).astype(BF16)
    ckv = _rms(z_mla[:, Q_LORA:Q_LORA + KV_LORA], g_ckv_ref[...]).astype(BF16)
    kp = _rms(z_mla[:, Q_LORA + KV_LORA:], g_kp_ref[...]) * rope_tab
    kp = (kp + pltpu.roll(kp, MLA_ROPE, axis=1)).astype(BF16)

    qk_scale = math.log2(math.e) / math.sqrt(MLA_NOPE + MLA_ROPE)
    g_qn = g_qn_ref[...] * qk_scale
    q_tab = rope_tab * (g_qp_ref[...] * qk_scale)
    q_all = _dot(cq, wq_ref[...])
    kv_all = _dot(ckv, wkv_ref[...])
    g1_ref[...] = _sigmoid(_dot(h, w_gate_ref[:, d_model:2 * d_model])).astype(BF16)
    nope_w = MLA_HEADS * MLA_NOPE
    for hd in range(MLA_HEADS):
        ns = slice(hd * MLA_NOPE, (hd + 1) * MLA_NOPE)
        ps = slice(nope_w + hd * ROPE_SLOT, nope_w + (hd + 1) * ROPE_SLOT)
        q_ref[hd, :, :MLA_NOPE] = _rms(q_all[:, ns], g_qn).astype(BF16)
        q_ref[hd, :, MLA_NOPE:] = _rms(q_all[:, ps], q_tab).astype(BF16)
        k_ref[hd, :, :MLA_NOPE] = _rms(kv_all[:, ns], g_kn_ref[...]).astype(BF16)
        k_ref[hd, :, MLA_NOPE:] = kp
        v_ref[hd] = kv_all[:, nope_w + hd * MLA_V:nope_w + (hd + 1) * MLA_V].astype(BF16)

    z_qm = _dot(h, w_qm_ref[...])
    z_g0 = _dot(h, w_gate_ref[:, :d_model])
    u = _gelu_tanh(z_uv[:, :gm_w])
    gv = _gelu_tanh(z_uv[:, gm_w:])
    mu = jnp.mean(gv, axis=-1, keepdims=True)
    gc = gv - mu
    var = jnp.mean(gc * gc, axis=-1, keepdims=True)
    v_ln = (gc * lax.rsqrt(var + EPS) * g_ln_ref[...] + b_ln_ref[...]).astype(BF16)
    row = lax.broadcasted_iota(jnp.int32, (GM_CHUNK, GM_CHUNK), 0)
    col = lax.broadcasted_iota(jnp.int32, (GM_CHUNK, GM_CHUNK), 1)
    gw = gm_w // GM_GROUPS
    for g in range(GM_GROUPS):
        w_causal = jnp.where(row >= col, ws_ref[g], 0.0).astype(BF16)
        bias = bs_ref[:, g:g + 1]
        for c in range(tm // GM_CHUNK):
            rs = slice(c * GM_CHUNK, (c + 1) * GM_CHUNK)
            cs = slice(g * gw, (g + 1) * gw)
            mix_scr[rs, cs] = _dot(w_causal, v_ln[rs, cs]) + bias
    y_gm = _dot((u * mix_scr[...]).astype(BF16), w_ogm_ref[...])
    mpart = _sigmoid(z_g0) * y_gm

    g2 = _sigmoid(_dot(h, w_gate_ref[:, 2 * d_model:]))
    g_mq = g_mq_ref[...] * (1.0 / math.sqrt(MEM_HEAD_DIM))
    for hd in range(MEM_HEADS):
        sl = slice(hd * MEM_HEAD_DIM, (hd + 1) * MEM_HEAD_DIM)
        qh = _rms(z_qm[:, sl], g_mq).astype(BF16)
        s = _dot_nt(qh, kmem_ref[:, sl])
        p = jnp.exp(s - jnp.max(s, axis=-1, keepdims=True))
        l = jnp.sum(p, axis=-1, keepdims=True)
        att_scr[:, sl] = _dot(p.astype(BF16), vmem_ref[:, sl]) / l
    y_mem = _dot(att_scr[...].astype(BF16), w_omem_ref[...])

    mpart_ref[...] = (mpart + g2 * y_mem).astype(BF16)


def _attn_kernel(q_ref, k_ref, v_ref, o_ref, vext_scr):
    seq, vw = v_ref.shape
    blk = ATTN_BLOCK
    vext_scr[:, :vw] = v_ref[...]
    vext_scr[:, vw:] = jnp.ones((seq, vext_scr.shape[1] - vw), BF16)
    half = blk // 2
    r = lax.broadcasted_iota(jnp.int32, (half, half), 0)
    c = lax.broadcasted_iota(jnp.int32, (half, half), 1)
    causal = r >= c
    rowmax = lambda s: jnp.max(s, axis=-1, keepdims=True)

    def scores(i):
        lo, mid, hi = i * blk, i * blk + half, (i + 1) * blk
        s_aa = jnp.where(causal, _dot_nt(q_ref[lo:mid, :], k_ref[lo:mid, :]), -1e30)
        s_ba = _dot_nt(q_ref[mid:hi, :], k_ref[lo:mid, :])
        s_bb = jnp.where(causal, _dot_nt(q_ref[mid:hi, :], k_ref[mid:hi, :]), -1e30)
        s_f = _dot_nt(q_ref[lo:hi, :], k_ref[:lo, :]) if i > 0 else None
        return s_aa, s_ba, s_bb, s_f

    def finish(i, s_aa, s_ba, s_bb, s_f):
        lo, mid, hi = i * blk, i * blk + half, (i + 1) * blk
        m_a = rowmax(s_aa)
        m_b = jnp.maximum(rowmax(s_ba), rowmax(s_bb))
        if s_f is not None:
            m_f = rowmax(s_f)
            m_a = jnp.maximum(m_a, m_f[:half])
            m_b = jnp.maximum(m_b, m_f[half:])
        o_a = _dot(jnp.exp2(s_aa - m_a).astype(BF16), vext_scr[lo:mid, :])
        p_b = jnp.concatenate([jnp.exp2(s_ba - m_b), jnp.exp2(s_bb - m_b)], axis=1)
        o = jnp.concatenate([o_a, _dot(p_b.astype(BF16), vext_scr[lo:hi, :])], axis=0)
        if s_f is not None:
            m = jnp.concatenate([m_a, m_b], axis=0)
            o = o + _dot(jnp.exp2(s_f - m).astype(BF16), vext_scr[:lo, :])
        o_ref[lo:hi, :] = (o[:, :vw] / o[:, vw:]).astype(o_ref.dtype)

    order = list(reversed(range(seq // blk)))
    pending = [scores(i) for i in order[:ATTN_LOOKAHEAD]]
    for n, i in enumerate(order):
        if n + ATTN_LOOKAHEAD < len(order):
            pending.append(scores(order[n + ATTN_LOOKAHEAD]))
        finish(i, *pending.pop(0))


def _ffn_kernel(x_ref, a_ref, mpart_ref, g1_ref, w_omla_ref, w_out_ref, g_ffn_ref,
                w1_ref, w2_ref, o_ref):
    attn = jnp.concatenate([a_ref[hd] for hd in range(a_ref.shape[0])], axis=1)
    y_mla = _dot(attn, w_omla_ref[...])
    merged = mpart_ref[...].astype(F32) + g1_ref[...].astype(F32) * y_mla
    x1 = x_ref[...] + _dot(merged.astype(BF16), w_out_ref[...])
    h2 = _rms(x1, g_ffn_ref[...]).astype(BF16)
    acc = x1
    for c in range(w1_ref.shape[1] // FF_CHUNK):
        cs = slice(c * FF_CHUNK, (c + 1) * FF_CHUNK)
        t = jnp.maximum(_dot(h2, w1_ref[:, cs]), 0.0)
        acc = acc + _dot((t * t).astype(BF16), w2_ref[cs, :])
    o_ref[...] = acc


def _const_spec(shape):
    nd = len(shape)
    return pl.BlockSpec(shape, lambda *_: (0,) * nd, pipeline_mode=pl.Buffered(1))


def _params(sem):
    return pltpu.CompilerParams(dimension_semantics=sem,
                                vmem_limit_bytes=V7X_VMEM_LIMIT_BYTES)


def kernel(x, mem, positions, g_mix, w_in, g_cq, w_uq, g_ckv, w_ukv, g_q_nope, g_q_pe,
           g_k_nope, g_k_pe, g_gm_ln, b_gm_ln, w_spatial, b_spatial, g_mem, w_mem_kv,
           g_mq, g_mk, w_o_gm, w_o_mla, w_o_mem, w_out, g_ffn, w_ff1, w_ff2):
    B, S, D = x.shape
    M = mem.shape[1]
    T = B * S
    depth = g_mix.shape[0]
    gm_w = g_gm_ln.shape[-1]
    mem_w = MEM_HEADS * MEM_HEAD_DIM
    tm = TOKEN_TILE
    tiles_per_seq = S // tm
    assert S % tm == 0 and S % ATTN_BLOCK == 0 and (B * M) % MEM_TILE == 0

    inv_freq = (ROPE_BASE ** (-jnp.arange(0, MLA_ROPE, 2, dtype=F32) / MLA_ROPE)).reshape(-1, 1)
    pos = positions.reshape(T // tm, 1, tm)
    row = lambda a: a.reshape(1, -1)
    with_swap = lambda a: jnp.concatenate([a, _swap_halves(a)], axis=-1)

    xt = x.reshape(T, D)
    for l in range(depth):
        wi = w_in[l].astype(BF16)
        c0 = 2 * gm_w
        c1 = c0 + Q_LORA + KV_LORA + MLA_ROPE
        c2 = c1 + mem_w
        w_uv = wi[:, :c0]
        w_mla = jnp.concatenate([wi[:, c0:c1], _swap_halves(wi[:, c1 - MLA_ROPE:c1])], axis=1)
        w_qm = wi[:, c1:c2]
        w_gate = wi[:, c2:]
        wq3 = w_uq[l].reshape(Q_LORA, MLA_HEADS, MLA_NOPE + MLA_ROPE)
        wq = jnp.concatenate([wq3[:, :, :MLA_NOPE].reshape(Q_LORA, -1),
                              with_swap(wq3[:, :, MLA_NOPE:]).reshape(Q_LORA, -1)],
                             axis=1).astype(BF16)
        wkv3 = w_ukv[l].reshape(KV_LORA, MLA_HEADS, MLA_NOPE + MLA_V)
        wkv = jnp.concatenate([wkv3[:, :, :MLA_NOPE].reshape(KV_LORA, -1),
                               wkv3[:, :, MLA_NOPE:].reshape(KV_LORA, -1)], axis=1).astype(BF16)

        k_mem, v_mem = pl.pallas_call(
            _memkv_kernel,
            grid=(B * M // MEM_TILE,),
            in_specs=[pl.BlockSpec((MEM_TILE, D), lambda i: (i, 0)),
                      _const_spec((1, D)), _const_spec((D, 2 * mem_w)),
                      _const_spec((1, MEM_HEAD_DIM))],
            out_specs=[pl.BlockSpec((MEM_TILE, mem_w), lambda i: (i, 0))] * 2,
            out_shape=[jax.ShapeDtypeStruct((B * M, mem_w), BF16)] * 2,
            compiler_params=_params(("parallel",)),
            name="memkv",
        )(mem.reshape(B * M, D), row(g_mem[l]), w_mem_kv[l].astype(BF16), row(g_mk[l]))
        k_mem = k_mem.reshape(B, M, mem_w)
        v_mem = v_mem.reshape(B, M, mem_w)

        tile_map = lambda i: (i, 0)
        head_map = lambda i: (i // tiles_per_seq, 0, i % tiles_per_seq, 0)
        mem_map = lambda i: (i // tiles_per_seq, 0, 0)
        consts = [
            inv_freq, row(g_mix[l]), w_uv, w_mla, w_qm, w_gate,
            row(g_gm_ln[l]), row(b_gm_ln[l]), w_spatial[l], b_spatial[l].T,
            w_o_gm[l].astype(BF16),
            row(g_cq[l]), wq, row(g_ckv[l]), wkv,
            row(g_q_nope[l]), row(with_swap(g_q_pe[l])),
            row(g_k_nope[l]), row(with_swap(g_k_pe[l])),
        ]
        tail_consts = [row(g_mq[l]), w_o_mem[l].astype(BF16)]
        q, k, v, mpart, g1 = pl.pallas_call(
            _mixer_kernel,
            grid=(T // tm,),
            in_specs=([pl.BlockSpec((tm, D), tile_map),
                       pl.BlockSpec((None, 1, tm), lambda i: (i, 0, 0))]
                      + [_const_spec(c.shape) for c in consts]
                      + [pl.BlockSpec((None, M, mem_w), mem_map)] * 2
                      + [_const_spec(c.shape) for c in tail_consts]),
            out_specs=[pl.BlockSpec((None, MLA_HEADS, tm, QK_WIDTH), head_map),
                       pl.BlockSpec((None, MLA_HEADS, tm, QK_WIDTH), head_map),
                       pl.BlockSpec((None, MLA_HEADS, tm, MLA_V), head_map),
                       pl.BlockSpec((tm, D), tile_map),
                       pl.BlockSpec((tm, D), tile_map)],
            out_shape=[jax.ShapeDtypeStruct((B, MLA_HEADS, S, QK_WIDTH), BF16),
                       jax.ShapeDtypeStruct((B, MLA_HEADS, S, QK_WIDTH), BF16),
                       jax.ShapeDtypeStruct((B, MLA_HEADS, S, MLA_V), BF16),
                       jax.ShapeDtypeStruct((T, D), BF16),
                       jax.ShapeDtypeStruct((T, D), BF16)],
            scratch_shapes=[pltpu.VMEM((tm, gm_w), F32), pltpu.VMEM((tm, mem_w), F32)],
            compiler_params=_params(("parallel",)),
            name="mixer",
        )(xt, pos, *consts, k_mem, v_mem, *tail_consts)

        bh_map = lambda b, h: (b, h, 0, 0)
        attn = pl.pallas_call(
            _attn_kernel,
            grid=(B, MLA_HEADS),
            in_specs=[pl.BlockSpec((None, None, S, QK_WIDTH), bh_map),
                      pl.BlockSpec((None, None, S, QK_WIDTH), bh_map),
                      pl.BlockSpec((None, None, S, MLA_V), bh_map)],
            out_specs=pl.BlockSpec((None, None, S, MLA_V), bh_map),
            out_shape=jax.ShapeDtypeStruct((B, MLA_HEADS, S, MLA_V), BF16),
            scratch_shapes=[pltpu.VMEM((S, 2 * MLA_V), BF16)],
            compiler_params=_params(("parallel", "parallel")),
            name="attn",
        )(q, k, v)

        ffn_consts = [w_o_mla[l].astype(BF16), w_out[l].astype(BF16), row(g_ffn[l]),
                      w_ff1[l].astype(BF16), w_ff2[l].astype(BF16)]
        xt = pl.pallas_call(
            _ffn_kernel,
            grid=(T // tm,),
            in_specs=([pl.BlockSpec((tm, D), tile_map),
                       pl.BlockSpec((None, MLA_HEADS, tm, MLA_V), head_map)]
                      + [pl.BlockSpec((tm, D), tile_map)] * 2
                      + [_const_spec(c.shape) for c in ffn_consts]),
            out_specs=pl.BlockSpec((tm, D), tile_map),
            out_shape=jax.ShapeDtypeStruct((T, D), F32),
            compiler_params=_params(("parallel",)),
            name="ffn",
        )(xt, attn, mpart, g1, *ffn_consts)
    return xt.reshape(B, S, D)
```

```python
import math

import jax
import jax.numpy as jnp
from jax import lax
from jax.experimental import pallas as pl
from jax.experimental.pallas import tpu as pltpu

F32 = jnp.float32
BF16 = jnp.bfloat16

EPS = 1e-6
ROPE_BASE = 10000.0

MEM_HEADS = 4
MEM_HEAD_DIM = 128
GM_CHUNK = 128
GM_GROUPS = 4
MLA_HEADS = 8
MLA_NOPE = 128
MLA_ROPE = 64
MLA_V = 128
Q_LORA = 384
KV_LORA = 256
N_BRANCH = 3

LANE = 128
V7X_VMEM_LIMIT_BYTES = 56 * 1024 * 1024

ROPE_SLOT = 2 * MLA_ROPE
assert ROPE_SLOT == LANE
QK_WIDTH = MLA_NOPE + ROPE_SLOT

TOKEN_TILE = 512
MEM_TILE = 1024
ATTN_BLOCK = 512
ATTN_LOOKAHEAD = 1
FF_CHUNK = 1024


def _rms(x, g):
    ms = jnp.sum(x * x, axis=-1, keepdims=True) * (1.0 / x.shape[-1])
    return x * lax.rsqrt(ms + EPS) * g


def _gelu_tanh(x):
    c = math.sqrt(2.0 / math.pi)
    cdf = 0.5 * (1.0 + jnp.tanh(c * (x + 0.044715 * (x * x * x))))
    return x * cdf


def _sigmoid(x):
    return 0.5 * jnp.tanh(0.5 * x) + 0.5


def _dot(a, b):
    return jnp.dot(a, b, preferred_element_type=F32)


def _dot_nt(a, b):
    return lax.dot_general(a, b, (((1,), (1,)), ((), ())), preferred_element_type=F32)


def _swap_halves(a):
    half = a.shape[-1] // 2
    return jnp.concatenate([a[..., half:], a[..., :half]], axis=-1)


def _memkv_kernel(mem_ref, g_mem_ref, w_ref, g_mk_ref, k_ref, v_ref):
    h = _rms(mem_ref[...], g_mem_ref[...]).astype(BF16)
    kv = _dot(h, w_ref[...])
    kw = MEM_HEADS * MEM_HEAD_DIM
    for hd in range(MEM_HEADS):
        sl = slice(hd * MEM_HEAD_DIM, (hd + 1) * MEM_HEAD_DIM)
        k_ref[:, sl] = _rms(kv[:, sl], g_mk_ref[...]).astype(BF16)
    v_ref[...] = kv[:, kw:].astype(BF16)


def _mixer_kernel(x_ref, pos_ref, invf_ref, g_mix_ref,
                  w_uv_ref, w_mla_ref, w_qm_ref, w_gate_ref,
                  g_ln_ref, b_ln_ref, ws_ref, bs_ref, w_ogm_ref,
                  g_cq_ref, wq_ref, g_ckv_ref, wkv_ref,
                  g_qn_ref, g_qp_ref, g_kn_ref, g_kp_ref,
                  kmem_ref, vmem_ref, g_mq_ref, w_omem_ref,
                  q_ref, k_ref, v_ref, mpart_ref, g1_ref,
                  mix_scr, att_scr):
    tm = x_ref.shape[0]
    gm_w = g_ln_ref.shape[-1]
    d_model = x_ref.shape[-1]

    h = _rms(x_ref[...], g_mix_ref[...]).astype(BF16)

    z_mla = _dot(h, w_mla_ref[...])
    z_uv = _dot(h, w_uv_ref[...])
    ang = invf_ref[...] * pos_ref[...].astype(F32)
    cos_t = jnp.cos(ang)
    sin_t = jnp.sin(ang)
    rope_tab = jnp.concatenate([cos_t, cos_t, -sin_t, sin_t], axis=0).T

    cq = _rms(z_mla[:, :Q_LORA], g_cq_ref[...]).astype(BF16)
    ckv = _rms(z_mla[:, Q_LORA:Q_LORA + KV_LORA], g_ckv_ref[...]).astype(BF16)
    kp = _rms(z_mla[:, Q_LORA + KV_LORA:], g_kp_ref[...]) * rope_tab
    kp = (kp + pltpu.roll(kp, MLA_ROPE, axis=1)).astype(BF16)

    qk_scale = math.log2(math.e) / math.sqrt(MLA_NOPE + MLA_ROPE)
    g_qn = g_qn_ref[...] * qk_scale
    q_tab = rope_tab * (g_qp_ref[...] * qk_scale)
    q_all = _dot(cq, wq_ref[...])
    kv_all = _dot(ckv, wkv_ref[...])
    g1_ref[...] = _sigmoid(_dot(h, w_gate_ref[:, d_model:2 * d_model])).astype(BF16)
    nope_w = MLA_HEADS * MLA_NOPE
    lane_id = lax.broadcasted_iota(jnp.int32, (tm, LANE), 1)
    ss_all = jnp.zeros((tm, LANE), F32)
    for hd in range(MLA_HEADS):
        kx = kv_all[:, hd * MLA_NOPE:(hd + 1) * MLA_NOPE]
        ss_all = jnp.where(lane_id == hd, jnp.sum(kx * kx, axis=-1, keepdims=True), ss_all)
    r_all = lax.rsqrt(ss_all * (1.0 / MLA_NOPE) + EPS)
    for hd in range(MLA_HEADS):
        ns = slice(hd * MLA_NOPE, (hd + 1) * MLA_NOPE)
        ps = slice(nope_w + hd * ROPE_SLOT, nope_w + (hd + 1) * ROPE_SLOT)
        q_ref[hd, :, :MLA_NOPE] = _rms(q_all[:, ns], g_qn).astype(BF16)
        q_ref[hd, :, MLA_NOPE:] = _rms(q_all[:, ps], q_tab).astype(BF16)
        r_hd = jnp.sum(jnp.where(lane_id == hd, r_all, 0.0), axis=-1, keepdims=True)
        k_ref[hd, :, :MLA_NOPE] = (kv_all[:, ns] * r_hd * g_kn_ref[...]).astype(BF16)
        k_ref[hd, :, MLA_NOPE:] = kp
        v_ref[hd] = kv_all[:, nope_w + hd * MLA_V:nope_w + (hd + 1) * MLA_V].astype(BF16)

    z_qm = _dot(h, w_qm_ref[...])
    z_g0 = _dot(h, w_gate_ref[:, :d_model])
    u = _gelu_tanh(z_uv[:, :gm_w])
    gv = _gelu_tanh(z_uv[:, gm_w:])
    mu = jnp.mean(gv, axis=-1, keepdims=True)
    gc = gv - mu
    var = jnp.mean(gc * gc, axis=-1, keepdims=True)
    v_ln = (gc * lax.rsqrt(var + EPS) * g_ln_ref[...] + b_ln_ref[...]).astype(BF16)
    row = lax.broadcasted_iota(jnp.int32, (GM_CHUNK, GM_CHUNK), 0)
    col = lax.broadcasted_iota(jnp.int32, (GM_CHUNK, GM_CHUNK), 1)
    gw = gm_w // GM_GROUPS
    for g in range(GM_GROUPS):
        w_causal = jnp.where(row >= col, ws_ref[g], 0.0).astype(BF16)
        bias = bs_ref[:, g:g + 1]
        for c in range(tm // GM_CHUNK):
            rs = slice(c * GM_CHUNK, (c + 1) * GM_CHUNK)
            cs = slice(g * gw, (g + 1) * gw)
            mix_scr[rs, cs] = _dot(w_causal, v_ln[rs, cs]) + bias
    y_gm = _dot((u * mix_scr[...]).astype(BF16), w_ogm_ref[...])
    mpart = _sigmoid(z_g0) * y_gm

    g2 = _sigmoid(_dot(h, w_gate_ref[:, 2 * d_model:]))
    g_mq = g_mq_ref[...] * (1.0 / math.sqrt(MEM_HEAD_DIM))
    for hd in range(MEM_HEADS):
        sl = slice(hd * MEM_HEAD_DIM, (hd + 1) * MEM_HEAD_DIM)
        qh = _rms(z_qm[:, sl], g_mq).astype(BF16)
        s = _dot_nt(qh, kmem_ref[:, sl])
        p = jnp.exp(s - jnp.max(s, axis=-1, keepdims=True))
        l = jnp.sum(p, axis=-1, keepdims=True)
        att_scr[:, sl] = _dot(p.astype(BF16), vmem_ref[:, sl]) / l
    y_mem = _dot(att_scr[...].astype(BF16), w_omem_ref[...])

    mpart_ref[...] = (mpart + g2 * y_mem).astype(BF16)


def _attn_kernel(q_ref, k_ref, v_ref, o_ref, vext_scr):
    seq, vw = v_ref.shape
    blk = ATTN_BLOCK
    vext_scr[:, :vw] = v_ref[...]
    vext_scr[:, vw:] = jnp.ones((seq, vext_scr.shape[1] - vw), BF16)
    half = blk // 2
    r = lax.broadcasted_iota(jnp.int32, (half, half), 0)
    c = lax.broadcasted_iota(jnp.int32, (half, half), 1)
    causal = r >= c
    rowmax = lambda s: jnp.max(s, axis=-1, keepdims=True)

    def scores(i):
        lo, mid, hi = i * blk, i * blk + half, (i + 1) * blk
        s_aa = jnp.where(causal, _dot_nt(q_ref[lo:mid, :], k_ref[lo:mid, :]), -1e30)
        s_ba = _dot_nt(q_ref[mid:hi, :], k_ref[lo:mid, :])
        s_bb = jnp.where(causal, _dot_nt(q_ref[mid:hi, :], k_ref[mid:hi, :]), -1e30)
        s_f = _dot_nt(q_ref[lo:hi, :], k_ref[:lo, :]) if i > 0 else None
        return s_aa, s_ba, s_bb, s_f

    def finish(i, s_aa, s_ba, s_bb, s_f):
        lo, mid, hi = i * blk, i * blk + half, (i + 1) * blk
        m_a = rowmax(s_aa)
        m_b = jnp.maximum(rowmax(s_ba), rowmax(s_bb))
        if s_f is not None:
            m_f = rowmax(s_f)
            m_a = jnp.maximum(m_a, m_f[:half])
            m_b = jnp.maximum(m_b, m_f[half:])
        o_a = _dot(jnp.exp2(s_aa - m_a).astype(BF16), vext_scr[lo:mid, :])
        p_b = jnp.concatenate([jnp.exp2(s_ba - m_b), jnp.exp2(s_bb - m_b)], axis=1)
        o = jnp.concatenate([o_a, _dot(p_b.astype(BF16), vext_scr[lo:hi, :])], axis=0)
        if s_f is not None:
            m = jnp.concatenate([m_a, m_b], axis=0)
            o = o + _dot(jnp.exp2(s_f - m).astype(BF16), vext_scr[:lo, :])
        o_ref[lo:hi, :] = (o[:, :vw] / o[:, vw:]).astype(o_ref.dtype)

    order = list(reversed(range(seq // blk)))
    pending = [scores(i) for i in order[:ATTN_LOOKAHEAD]]
    for n, i in enumerate(order):
        if n + ATTN_LOOKAHEAD < len(order):
            pending.append(scores(order[n + ATTN_LOOKAHEAD]))
        finish(i, *pending.pop(0))


def _ffn_kernel(x_ref, a_ref, mpart_ref, g1_ref, w_omla_ref, w_out_ref, g_ffn_ref,
                w1_ref, w2_ref, o_ref):
    attn = jnp.concatenate([a_ref[hd] for hd in range(a_ref.shape[0])], axis=1)
    y_mla = _dot(attn, w_omla_ref[...])
    merged = mpart_ref[...].astype(F32) + g1_ref[...].astype(F32) * y_mla
    x1 = x_ref[...] + _dot(merged.astype(BF16), w_out_ref[...])
    h2 = _rms(x1, g_ffn_ref[...]).astype(BF16)
    acc = x1
    for c in range(w1_ref.shape[1] // FF_CHUNK):
        cs = slice(c * FF_CHUNK, (c + 1) * FF_CHUNK)
        t = jnp.maximum(_dot(h2, w1_ref[:, cs]), 0.0)
        acc = acc + _dot((t * t).astype(BF16), w2_ref[cs, :])
    o_ref[...] = acc


def _const_spec(shape):
    nd = len(shape)
    return pl.BlockSpec(shape, lambda *_: (0,) * nd, pipeline_mode=pl.Buffered(1))


def _params(sem):
    return pltpu.CompilerParams(dimension_semantics=sem,
                                vmem_limit_bytes=V7X_VMEM_LIMIT_BYTES)


def kernel(x, mem, positions, g_mix, w_in, g_cq, w_uq, g_ckv, w_ukv, g_q_nope, g_q_pe,
           g_k_nope, g_k_pe, g_gm_ln, b_gm_ln, w_spatial, b_spatial, g_mem, w_mem_kv,
           g_mq, g_mk, w_o_gm, w_o_mla, w_o_mem, w_out, g_ffn, w_ff1, w_ff2):
    B, S, D = x.shape
    M = mem.shape[1]
    T = B * S
    depth = g_mix.shape[0]
    gm_w = g_gm_ln.shape[-1]
    mem_w = MEM_HEADS * MEM_HEAD_DIM
    tm = TOKEN_TILE
    tiles_per_seq = S // tm
    assert S % tm == 0 and S % ATTN_BLOCK == 0 and (B * M) % MEM_TILE == 0

    inv_freq = (ROPE_BASE ** (-jnp.arange(0, MLA_ROPE, 2, dtype=F32) / MLA_ROPE)).reshape(-1, 1)
    pos = positions.reshape(T // tm, 1, tm)
    row = lambda a: a.reshape(1, -1)
    with_swap = lambda a: jnp.concatenate([a, _swap_halves(a)], axis=-1)

    xt = x.reshape(T, D)
    for l in range(depth):
        wi = w_in[l].astype(BF16)
        c0 = 2 * gm_w
        c1 = c0 + Q_LORA + KV_LORA + MLA_ROPE
        c2 = c1 + mem_w
        w_uv = wi[:, :c0]
        w_mla = jnp.concatenate([wi[:, c0:c1], _swap_halves(wi[:, c1 - MLA_ROPE:c1])], axis=1)
        w_qm = wi[:, c1:c2]
        w_gate = wi[:, c2:]
        wq3 = w_uq[l].reshape(Q_LORA, MLA_HEADS, MLA_NOPE + MLA_ROPE)
        wq = jnp.concatenate([wq3[:, :, :MLA_NOPE].reshape(Q_LORA, -1),
                              with_swap(wq3[:, :, MLA_NOPE:]).reshape(Q_LORA, -1)],
                             axis=1).astype(BF16)
        wkv3 = w_ukv[l].reshape(KV_LORA, MLA_HEADS, MLA_NOPE + MLA_V)
        wkv = jnp.concatenate([wkv3[:, :, :MLA_NOPE].reshape(KV_LORA, -1),
                               wkv3[:, :, MLA_NOPE:].reshape(KV_LORA, -1)], axis=1).astype(BF16)

        k_mem, v_mem = pl.pallas_call(
            _memkv_kernel,
            grid=(B * M // MEM_TILE,),
            in_specs=[pl.BlockSpec((MEM_TILE, D), lambda i: (i, 0)),
                      _const_spec((1, D)), _const_spec((D, 2 * mem_w)),
                      _const_spec((1, MEM_HEAD_DIM))],
            out_specs=[pl.BlockSpec((MEM_TILE, mem_w), lambda i: (i, 0))] * 2,
            out_shape=[jax.ShapeDtypeStruct((B * M, mem_w), BF16)] * 2,
            compiler_params=_params(("parallel",)),
            name="memkv",
        )(mem.reshape(B * M, D), row(g_mem[l]), w_mem_kv[l].astype(BF16), row(g_mk[l]))
        k_mem = k_mem.reshape(B, M, mem_w)
        v_mem = v_mem.reshape(B, M, mem_w)

        tile_map = lambda i: (i, 0)
        head_map = lambda i: (i // tiles_per_seq, 0, i % tiles_per_seq, 0)
        mem_map = lambda i: (i // tiles_per_seq, 0, 0)
        consts = [
            inv_freq, row(g_mix[l]), w_uv, w_mla, w_qm, w_gate,
            row(g_gm_ln[l]), row(b_gm_ln[l]), w_spatial[l], b_spatial[l].T,
            w_o_gm[l].astype(BF16),
            row(g_cq[l]), wq, row(g_ckv[l]), wkv,
            row(g_q_nope[l]), row(with_swap(g_q_pe[l])),
            row(g_k_nope[l]), row(with_swap(g_k_pe[l])),
        ]
        tail_consts = [row(g_mq[l]), w_o_mem[l].astype(BF16)]
        q, k, v, mpart, g1 = pl.pallas_call(
            _mixer_kernel,
            grid=(T // tm,),
            in_specs=([pl.BlockSpec((tm, D), tile_map),
                       pl.BlockSpec((None, 1, tm), lambda i: (i, 0, 0))]
                      + [_const_spec(c.shape) for c in consts]
                      + [pl.BlockSpec((None, M, mem_w), mem_map)] * 2
                      + [_const_spec(c.shape) for c in tail_consts]),
            out_specs=[pl.BlockSpec((None, MLA_HEADS, tm, QK_WIDTH), head_map),
                       pl.BlockSpec((None, MLA_HEADS, tm, QK_WIDTH), head_map),
                       pl.BlockSpec((None, MLA_HEADS, tm, MLA_V), head_map),
                       pl.BlockSpec((tm, D), tile_map),
                       pl.BlockSpec((tm, D), tile_map)],
            out_shape=[jax.ShapeDtypeStruct((B, MLA_HEADS, S, QK_WIDTH), BF16),
                       jax.ShapeDtypeStruct((B, MLA_HEADS, S, QK_WIDTH), BF16),
                       jax.ShapeDtypeStruct((B, MLA_HEADS, S, MLA_V), BF16),
                       jax.ShapeDtypeStruct((T, D), BF16),
                       jax.ShapeDtypeStruct((T, D), BF16)],
            scratch_shapes=[pltpu.VMEM((tm, gm_w), F32), pltpu.VMEM((tm, mem_w), F32)],
            compiler_params=_params(("parallel",)),
            name="mixer",
        )(xt, pos, *consts, k_mem, v_mem, *tail_consts)

        bh_map = lambda b, h: (b, h, 0, 0)
        attn = pl.pallas_call(
            _attn_kernel,
            grid=(B, MLA_HEADS),
            in_specs=[pl.BlockSpec((None, None, S, QK_WIDTH), bh_map),
                      pl.BlockSpec((None, None, S, QK_WIDTH), bh_map),
                      pl.BlockSpec((None, None, S, MLA_V), bh_map)],
            out_specs=pl.BlockSpec((None, None, S, MLA_V), bh_map),
            out_shape=jax.ShapeDtypeStruct((B, MLA_HEADS, S, MLA_V), BF16),
            scratch_shapes=[pltpu.VMEM((S, 2 * MLA_V), BF16)],
            compiler_params=_params(("parallel", "parallel")),
            name="attn",
        )(q, k, v)

        ffn_consts = [w_o_mla[l].astype(BF16), w_out[l].astype(BF16), row(g_ffn[l]),
                      w_ff1[l].astype(BF16), w_ff2[l].astype(BF16)]
        xt = pl.pallas_call(
            _ffn_kernel,
            grid=(T // tm,),
            in_specs=([pl.BlockSpec((tm, D), tile_map),
                       pl.BlockSpec((None, MLA_HEADS, tm, MLA_V), head_map)]
                      + [pl.BlockSpec((tm, D), tile_map)] * 2
                      + [_const_spec(c.shape) for c in ffn_consts]),
            out_specs=pl.BlockSpec((tm, D), tile_map),
            out_shape=jax.ShapeDtypeStruct((T, D), F32),
            compiler_params=_params(("parallel",)),
            name="ffn",
        )(xt, attn, mpart, g1, *ffn_consts)
    return xt.reshape(B, S, D)
```
